```python
import jax, jax.numpy as jnp
from jax import lax
import numpy as np

D_MODEL = 1024
BATCH = 4
SEQ = 4096
DEPTH = 2
DEC_BATCH = 32
DEC_SEQ = 1
PAST_LEN = 16384
PAGE_SIZE = 128

N_HEADS = 16
HEAD_DIM = D_MODEL // N_HEADS
D_FF = 2816
CONV_W = 3
N_A_LAYERS = DEPTH // 2
N_B_LAYERS = DEPTH - N_A_LAYERS
DILATED = ((128, 1), (512, 4), (2048, 16))
W_MAX = 2048
Q_BLOCK = 128
ROPE_THETA = 10000.0
NORM_EPS = 1e-6
FORGET_BIAS_LO = 2.0
FORGET_BIAS_HI = 10.0
NEG_INF = -1e30

kernel_name = 'yoco_fox_dilated_convffn_step'


def forget_bias_schedule():
    return jnp.linspace(FORGET_BIAS_LO, FORGET_BIAS_HI, N_HEADS, dtype=jnp.float32)


def rmsnorm(x, g):
    xf = x.astype(jnp.float32)
    y = xf * lax.rsqrt(jnp.mean(xf * xf, axis=-1, keepdims=True) + NORM_EPS)
    return (y * g.astype(jnp.float32)).astype(x.dtype)


def rope(x, pos):
    half = HEAD_DIM // 2
    inv = jnp.power(ROPE_THETA, -jnp.arange(half, dtype=jnp.float32) * (2.0 / HEAD_DIM))
    ang = pos.astype(jnp.float32)[:, None] * inv[None, :]
    cos = jnp.cos(ang)[None, :, None, :]
    sin = jnp.sin(ang)[None, :, None, :]
    xf = x.astype(jnp.float32)
    x1, x2 = xf[..., :half], xf[..., half:]
    return jnp.concatenate([x1 * cos - x2 * sin, x2 * cos + x1 * sin], axis=-1).astype(x.dtype)


def modulation(c, w_mod_l, b_mod_l):
    m = jax.nn.silu(c) @ w_mod_l + b_mod_l
    return jnp.split(m[:, None, :], 6, axis=-1)


def fox_project(u, w_in, b_f):
    b, t, _ = u.shape
    q, k, v, f = jnp.split(u @ w_in, [D_MODEL, 2 * D_MODEL, 3 * D_MODEL], axis=-1)
    shp = (b, t, N_HEADS, HEAD_DIM)
    logf = jax.nn.log_sigmoid((f + b_f).astype(jnp.float32))
    return q.reshape(shp), k.reshape(shp), v.reshape(shp), logf


def fox_prompt(q, k, v, logf):
    b, s = q.shape[:2]
    scale = HEAD_DIM ** -0.5
    dcum = jnp.cumsum(logf, axis=1).transpose(0, 2, 1)
    kpos = jnp.arange(s)

    def block(i):
        t0 = i * Q_BLOCK
        qb = lax.dynamic_slice_in_dim(q, t0, Q_BLOCK, axis=1)
        db = lax.dynamic_slice_in_dim(dcum, t0, Q_BLOCK, axis=2)
        logits = jnp.einsum('bqhd,bshd->bhqs', qb, k).astype(jnp.float32) * scale
        logits = logits + db[..., :, None] - dcum[..., None, :]
        qpos = t0 + jnp.arange(Q_BLOCK)
        logits = jnp.where(kpos[None, :] <= qpos[:, None], logits, NEG_INF)
        p = jax.nn.softmax(logits, axis=-1).astype(v.dtype)
        return jnp.einsum('bhqs,bshd->bqhd', p, v)

    o = lax.map(block, jnp.arange(s // Q_BLOCK))
    return o.transpose(1, 0, 2, 3, 4).reshape(b, s, D_MODEL)


def fox_sample(q, k, v, logf, k_past, v_past, logf_past):
    b, n = q.shape[:2]
    p_len = k_past.shape[1]
    scale = HEAD_DIM ** -0.5
    lfp = logf_past.astype(jnp.float32)
    r = (lax.cumsum(lfp, axis=1, reverse=True) - lfp).transpose(0, 2, 1)
    dn = jnp.cumsum(logf, axis=1).transpose(0, 2, 1)
    s_past = jnp.einsum('bqhd,bshd->bhqs', q, k_past).astype(jnp.float32) * scale
    s_past = s_past + r[..., None, :] + dn[..., :, None]
    s_new = jnp.einsum('bqhd,bshd->bhqs', q, k).astype(jnp.float32) * scale
    s_new = s_new + dn[..., :, None] - dn[..., None, :]
    causal = jnp.arange(n)[None, :] <= jnp.arange(n)[:, None]
    s_new = jnp.where(causal, s_new, NEG_INF)
    p = jax.nn.softmax(jnp.concatenate([s_past, s_new], axis=-1), axis=-1).astype(v.dtype)
    o = (jnp.einsum('bhqs,bshd->bqhd', p[..., :p_len], v_past)
         + jnp.einsum('bhqs,bshd->bqhd', p[..., p_len:], v))
    return o.reshape(b, n, D_MODEL)


def dilated_attend(q, k_src, v_src, base):
    b, nq = q.shape[:2]
    scale = HEAD_DIM ** -0.5
    outs, lses = [], []
    for window, dil in DILATED:
        n_keys = window // dil + 1
        idx = base[:, None] - dil * jnp.arange(n_keys)[None, :]
        valid = idx >= 0
        flat = jnp.maximum(idx, 0).reshape(-1)
        kg = jnp.take(k_src, flat, axis=1).reshape(b, nq, n_keys, N_HEADS, HEAD_DIM)
        vg = jnp.take(v_src, flat, axis=1).reshape(b, nq, n_keys, N_HEADS, HEAD_DIM)
        s = jnp.einsum('bqhd,bqjhd->bhqj', q, kg).astype(jnp.float32) * scale
        s = jnp.where(valid[None, None], s, NEG_INF)
        lse = jax.nn.logsumexp(s, axis=-1)
        p = jnp.exp(s - lse[..., None]).astype(v_src.dtype)
        outs.append(jnp.einsum('bhqj,bqjhd->bqhd', p, vg))
        lses.append(lse)
    wts = jax.nn.softmax(jnp.stack(lses), axis=0).astype(q.dtype)
    o = jnp.einsum('gbhq,gbqhd->bqhd', wts, jnp.stack(outs))
    return o.reshape(b, nq, D_MODEL)


def dilated_prompt(q, k, v):
    b, s = q.shape[:2]

    def block(i):
        t0 = i * Q_BLOCK
        qb = lax.dynamic_slice_in_dim(q, t0, Q_BLOCK, axis=1)
        return dilated_attend(qb, k, v, t0 + jnp.arange(Q_BLOCK))

    o = lax.map(block, jnp.arange(s // Q_BLOCK))
    return o.transpose(1, 0, 2, 3).reshape(b, s, D_MODEL)


def shared_kv(h, g_kv, w_kv, pos):
    b, t, _ = h.shape
    k, v = jnp.split(rmsnorm(h, g_kv) @ w_kv, 2, axis=-1)
    k = rope(k.reshape(b, t, N_HEADS, HEAD_DIM), pos)
    return k, v.reshape(b, t, N_HEADS, HEAD_DIM)


def conv_ffn(u, prev, w_up, w_conv, b_conv, w_down):
    t = u.shape[1]
    a = u @ w_up
    ap = jnp.concatenate([prev.astype(a.dtype), a], axis=1)
    y = b_conv
    for j in range(CONV_W):
        y = y + w_conv[j] * ap[:, j:j + t]
    gate, val = jnp.split(y, 2, axis=-1)
    out = (jax.nn.gelu(gate, approximate=True) * val) @ w_down
    return out, ap[:, -(CONV_W - 1):]


def setup_inputs(seed: int = 0) -> dict:
    key = jax.random.key(seed)
    ks = jax.random.split(key, 32)
    f32 = jnp.float32

    def nrm(k, shape, s=1.0):
        return s * jax.random.normal(k, shape, f32)

    n_pages = PAST_LEN // PAGE_SIZE
    n_used = DEC_BATCH * n_pages
    n_pool = n_used + (n_used + 3) // 4
    wb = min(W_MAX, PAST_LEN)
    dm = D_MODEL
    fb = forget_bias_schedule()
    w_in_a = jnp.concatenate([nrm(ks[14], (N_A_LAYERS, dm, 3 * dm), dm ** -0.5),
                              nrm(ks[25], (N_A_LAYERS, dm, N_HEADS), 0.1 * dm ** -0.5)], axis=-1)
    return {
        'x_prompt': nrm(ks[0], (BATCH, SEQ, dm)),
        'x_sample': nrm(ks[1], (DEC_BATCH, DEC_SEQ, dm)),
        'cache_k_a': nrm(ks[2], (N_A_LAYERS, n_pool, PAGE_SIZE, N_HEADS, HEAD_DIM)),
        'cache_v_a': nrm(ks[3], (N_A_LAYERS, n_pool, PAGE_SIZE, N_HEADS, HEAD_DIM)),
        'cache_logf_a': jax.nn.log_sigmoid(fb + nrm(ks[4], (N_A_LAYERS, n_pool, PAGE_SIZE, N_HEADS), 0.5)),
        'page_table': jax.random.permutation(ks[5], n_pool)[:n_used].reshape(DEC_BATCH, n_pages).astype(jnp.int32),
        'cache_k_b': nrm(ks[6], (DEC_BATCH, wb, N_HEADS, HEAD_DIM)),
        'cache_v_b': nrm(ks[7], (DEC_BATCH, wb, N_HEADS, HEAD_DIM)),
        'state_conv': nrm(ks[8], (DEPTH, DEC_BATCH, CONV_W - 1, 2 * D_FF)),
        'c_prompt': nrm(ks[9], (BATCH, dm)),
        'c_sample': nrm(ks[10], (DEC_BATCH, dm)),
        'w_mod': nrm(ks[11], (DEPTH, dm, 6 * dm), 0.5 * dm ** -0.5),
        'b_mod': nrm(ks[12], (DEPTH, 6 * dm), 0.02),
        'g_norm': 1.0 + nrm(ks[13], (DEPTH, 4, dm), 0.05),
        'w_in_a': w_in_a,
        'b_f_a': fb + nrm(ks[15], (N_A_LAYERS, N_HEADS), 0.1),
        'w_out_a': nrm(ks[16], (N_A_LAYERS, dm, dm), dm ** -0.5),
        'g_kv': 1.0 + nrm(ks[17], (dm,), 0.05),
        'w_kv_b': nrm(ks[18], (dm, 2 * dm), dm ** -0.5),
        'w_q_b': nrm(ks[19], (N_B_LAYERS, dm, dm), dm ** -0.5),
        'w_out_b': nrm(ks[20], (N_B_LAYERS, dm, dm), dm ** -0.5),
        'w_up': nrm(ks[21], (DEPTH, dm, 2 * D_FF), dm ** -0.5),
        'w_conv': nrm(ks[22], (DEPTH, CONV_W, 2 * D_FF), CONV_W ** -0.5),
        'b_conv': nrm(ks[23], (DEPTH, 2 * D_FF), 0.02),
        'w_down': nrm(ks[24], (DEPTH, D_FF, dm), D_FF ** -0.5),
    }


def reference(x_prompt, x_sample, cache_k_a, cache_v_a, cache_logf_a, page_table, cache_k_b, cache_v_b,
              state_conv, c_prompt, c_sample, w_mod, b_mod, g_norm, w_in_a, b_f_a, w_out_a, g_kv,
              w_kv_b, w_q_b, w_out_b, w_up, w_conv, b_conv, w_down):
    bp, sp = x_prompt.shape[:2]
    bs, ns = x_sample.shape[:2]
    pos_p = jnp.arange(sp)
    pos_s = PAST_LEN + jnp.arange(ns)
    hp, hs = x_prompt, x_sample
    ka_p, va_p, lfa_p, ka_s, va_s, lfa_s, cv_p, cv_s = [], [], [], [], [], [], [], []
    kb_p = vb_p = kb_s = vb_s = None
    for l in range(DEPTH):
        sh_p, sc_p, gt_p, shf_p, scf_p, gtf_p = modulation(c_prompt, w_mod[l], b_mod[l])
        sh_s, sc_s, gt_s, shf_s, scf_s, gtf_s = modulation(c_sample, w_mod[l], b_mod[l])
        up_ = rmsnorm(hp, g_norm[l, 0]) * (1.0 + sc_p) + sh_p
        us_ = rmsnorm(hs, g_norm[l, 0]) * (1.0 + sc_s) + sh_s
        if l < N_A_LAYERS:
            q, k, v, lf = fox_project(up_, w_in_a[l], b_f_a[l])
            op = fox_prompt(q, k, v, lf) @ w_out_a[l]
            ka_p.append(k); va_p.append(v); lfa_p.append(lf)
            q, k, v, lf = fox_project(us_, w_in_a[l], b_f_a[l])
            k_past = cache_k_a[l, page_table].reshape(bs, -1, N_HEADS, HEAD_DIM)
            v_past = cache_v_a[l, page_table].reshape(bs, -1, N_HEADS, HEAD_DIM)
            lf_past = cache_logf_a[l, page_table].reshape(bs, -1, N_HEADS)
            os_ = fox_sample(q, k, v, lf, k_past, v_past, lf_past) @ w_out_a[l]
            ka_s.append(k); va_s.append(v); lfa_s.append(lf)
        else:
            lb = l - N_A_LAYERS
            qp = rope((up_ @ w_q_b[lb]).reshape(bp, sp, N_HEADS, HEAD_DIM), pos_p)
            op = dilated_prompt(qp, kb_p, vb_p) @ w_out_b[lb]
            qs = rope((us_ @ w_q_b[lb]).reshape(bs, ns, N_HEADS, HEAD_DIM), pos_s)
            k_src = jnp.concatenate([cache_k_b, kb_s], axis=1)
            v_src = jnp.concatenate([cache_v_b, vb_s], axis=1)
            base = cache_k_b.shape[1] + jnp.arange(ns)
            os_ = dilated_attend(qs, k_src, v_src, base) @ w_out_b[lb]
        hp = hp + gt_p * rmsnorm(op, g_norm[l, 1])
        hs = hs + gt_s * rmsnorm(os_, g_norm[l, 1])
        fp, cp = conv_ffn(rmsnorm(hp, g_norm[l, 2]) * (1.0 + scf_p) + shf_p,
                          jnp.zeros((bp, CONV_W - 1, 2 * D_FF), hp.dtype),
                          w_up[l], w_conv[l], b_conv[l], w_down[l])
        fs, cs = conv_ffn(rmsnorm(hs, g_norm[l, 2]) * (1.0 + scf_s) + shf_s, state_conv[l],
                          w_up[l], w_conv[l], b_conv[l], w_down[l])
        hp = hp + gtf_p * rmsnorm(fp, g_norm[l, 3])
        hs = hs + gtf_s * rmsnorm(fs, g_norm[l, 3])
        cv_p.append(cp); cv_s.append(cs)
        if l == N_A_LAYERS - 1:
            kb_p, vb_p = shared_kv(hp, g_kv, w_kv_b, pos_p)
            kb_s, vb_s = shared_kv(hs, g_kv, w_kv_b, pos_s)
    wp = min(W_MAX, sp)
    y_prompt = hp
    y_sample = hs
    new_k_a_prompt = jnp.stack(ka_p)
    new_v_a_prompt = jnp.stack(va_p)
    new_logf_a_prompt = jnp.stack(lfa_p)
    new_k_a_sample = jnp.stack(ka_s)
    new_v_a_sample = jnp.stack(va_s)
    new_logf_a_sample = jnp.stack(lfa_s)
    new_k_b_prompt = kb_p[:, sp - wp:]
    new_v_b_prompt = vb_p[:, sp - wp:]
    new_conv_prompt = jnp.stack(cv_p)
    new_conv_sample = jnp.stack(cv_s)
    return (y_prompt, y_sample, new_k_a_prompt, new_v_a_prompt, new_logf_a_prompt,
            new_k_a_sample, new_v_a_sample, new_logf_a_sample,
            new_k_b_prompt, new_v_b_prompt, kb_s, vb_s, new_conv_prompt, new_conv_sample)
```

```python
import functools

import jax
import jax.numpy as jnp
from jax import lax
from jax.experimental import pallas as pl
from jax.experimental.pallas import tpu as pltpu

F32 = jnp.float32
BF16 = jnp.bfloat16

N_HEADS = 16
HEAD_DIM = 64
DILATIONS = (1, 4, 16)
WINDOW_KEYS = 128
W_MAX = 2048
ROPE_THETA = 10000.0
NORM_EPS = 1e-6
NEG_INF = -1e30
SCALE = HEAD_DIM ** -0.5

LANES = 128
PAIR = 2 * HEAD_DIM
VMEM_LIMIT_BYTES = 56 * 1024 * 1024

_NT = (((1,), (1,)), ((), ()))
_TN = (((0,), (0,)), ((), ()))


def _params(n_grid_axes):
    return pltpu.CompilerParams(
        dimension_semantics=("arbitrary",) * n_grid_axes,
        vmem_limit_bytes=VMEM_LIMIT_BYTES,
    )


def _resident(shape):
    nd = len(shape)
    return pl.BlockSpec(shape, lambda *_: (0,) * nd, pipeline_mode=pl.Buffered(1))


def _rms_hat(x):
    return x * lax.rsqrt(jnp.mean(x * x, axis=-1, keepdims=True) + NORM_EPS)


def _log_sigmoid(x):
    return jnp.minimum(x, 0.0) - jnp.log1p(jnp.exp(-jnp.abs(x)))


def _gelu_tanh(x):
    return x * (0.5 * (1.0 + jnp.tanh(0.7978845608028654 * (x + 0.044715 * (x * x * x)))))


def _split3(x):
    hi = x.astype(BF16)
    r = x - hi.astype(F32)
    mid = r.astype(BF16)
    lo = (r - mid.astype(F32)).astype(BF16)
    return hi, mid, lo


def _dot(a, b):
    return jnp.dot(a, b, preferred_element_type=F32)


def _rope(x, cos, sin):
    d = x.shape[-1]
    lane = lax.broadcasted_iota(jnp.int32, x.shape, 1)
    first = (lane % HEAD_DIM) < (HEAD_DIM // 2)
    partner = jnp.where(first, pltpu.roll(x, d - HEAD_DIM // 2, 1), pltpu.roll(x, HEAD_DIM // 2, 1))
    reps = d // LANES
    return x * jnp.tile(cos, (1, reps)) + partner * jnp.tile(sin, (1, reps))


def _rope_tables(pos):
    half = HEAD_DIM // 2
    inv = jnp.power(ROPE_THETA, -jnp.arange(half, dtype=F32) * (2.0 / HEAD_DIM))
    ang = pos.astype(F32)[:, None] * inv[None, :]
    cos, sin = jnp.cos(ang), jnp.sin(ang)
    cos_h = jnp.concatenate([cos, cos], axis=-1)
    sin_h = jnp.concatenate([-sin, sin], axis=-1)
    return jnp.tile(cos_h, (1, LANES // HEAD_DIM)), jnp.tile(sin_h, (1, LANES // HEAD_DIM))


def _mod_body(c_ref, w_ref, b_ref, o_ref):
    c = c_ref[...]
    s = (c * jax.nn.sigmoid(c)).astype(BF16)
    o_ref[0] = _dot(s, w_ref[0].astype(BF16)) + b_ref[0]


def _modulation(c_all, w_mod, b_mod):
    depth, d, n = w_mod.shape
    r = c_all.shape[0]
    tn = 1536
    return pl.pallas_call(
        _mod_body,
        grid=(depth, n // tn),
        in_specs=[
            pl.BlockSpec((r, d), lambda l, j: (0, 0)),
            pl.BlockSpec((1, d, tn), lambda l, j: (l, 0, j)),
            pl.BlockSpec((1, 1, tn), lambda l, j: (l, 0, j)),
        ],
        out_specs=pl.BlockSpec((1, r, tn), lambda l, j: (l, 0, j)),
        out_shape=jax.ShapeDtypeStruct((depth, r, n), F32),
        compiler_params=_params(2),
        name="modulation",
    )(c_all, w_mod, b_mod.reshape(depth, 1, n))


def _fox_proj_prompt_body(x_ref, g_ref, sc_ref, sh_ref, w_ref, wf_ref, wft_ref, bf_ref, bft_ref, tri_ref,
                          q_ref, k_ref, v_ref, kf_ref, vf_ref, lf_ref, d_ref, carry_ref):
    i = pl.program_id(1)
    x = x_ref[0]
    tm, dm = x.shape
    u = _rms_hat(x) * g_ref[...] * (1.0 + sc_ref[0]) + sh_ref[0]
    ub = u.astype(BF16)
    qkv = _dot(ub, w_ref[...])
    k = qkv[:, dm:2 * dm]
    v = qkv[:, 2 * dm:]
    kf_ref[0] = k
    vf_ref[0] = v
    qb = (qkv[:, :dm] * SCALE).astype(BF16)
    kb = k.astype(BF16)
    vb = v.astype(BF16)
    for p in range(dm // PAIR):
        sl = slice(p * PAIR, (p + 1) * PAIR)
        q_ref[0, p] = qb[:, sl]
        k_ref[0, p] = kb[:, sl]
        v_ref[0, p] = vb[:, sl]
    lf_ref[0] = _log_sigmoid(_dot(ub, wf_ref[...]) + bf_ref[...])
    lft = _log_sigmoid(lax.dot_general(wft_ref[...], ub, _NT, preferred_element_type=F32) + bft_ref[...])
    tri = tri_ref[...]
    hi, mid, lo = _split3(lft)
    dloc = _dot(hi, tri) + _dot(mid, tri) + _dot(lo, tri)

    @pl.when(i == 0)
    def _():
        carry_ref[...] = jnp.zeros_like(carry_ref)

    dfull = dloc + carry_ref[...]
    carry_ref[...] = dfull[:, tm - 1:tm]
    for h in range(N_HEADS):
        d_ref[0, h] = dfull[h:h + 1, :]


def _fox_proj_prompt(x, g, sc, sh, w_qkv, w_f, b_f, tm=512):
    b, s, d = x.shape
    npair = d // PAIR
    nh = w_f.shape[1]
    tri = (jnp.arange(tm)[:, None] <= jnp.arange(tm)[None, :]).astype(BF16)
    pm_shape = jax.ShapeDtypeStruct((b, npair, s, PAIR), BF16)
    pm_spec = pl.BlockSpec((1, npair, tm, PAIR), lambda bi, i: (bi, 0, i, 0))
    row_spec = pl.BlockSpec((1, tm, d), lambda bi, i: (bi, i, 0))
    mod_spec = pl.BlockSpec((1, 1, d), lambda bi, i: (bi, 0, 0))
    return pl.pallas_call(
        _fox_proj_prompt_body,
        grid=(b, s // tm),
        in_specs=[
            row_spec, _resident((1, d)), mod_spec, mod_spec,
            _resident((d, 3 * d)), _resident((d, nh)), _resident((nh, d)),
            _resident((1, nh)), _resident((nh, 1)), _resident((tm, tm)),
        ],
        out_specs=[
            pm_spec, pm_spec, pm_spec, row_spec, row_spec,
            pl.BlockSpec((1, tm, nh), lambda bi, i: (bi, i, 0)),
            pl.BlockSpec((1, nh, 1, tm), lambda bi, i: (bi, 0, 0, i)),
        ],
        out_shape=[
            pm_shape, pm_shape, pm_shape,
            jax.ShapeDtypeStruct((b, s, d), F32), jax.ShapeDtypeStruct((b, s, d), F32),
            jax.ShapeDtypeStruct((b, s, nh), F32), jax.ShapeDtypeStruct((b, nh, 1, s), F32),
        ],
        scratch_shapes=[pltpu.VMEM((nh, 1), F32)],
        compiler_params=_params(2),
        name="fox_proj_prompt",
    )(x, g.reshape(1, d), sc, sh, w_qkv, w_f, w_f.T, b_f.reshape(1, nh), b_f.reshape(nh, 1), tri)


def _fox_proj_sample_body(x_ref, g_ref, sc_ref, sh_ref, w_ref, wf_ref, bf_ref, q_ref, k_ref, v_ref, lf_ref):
    x = x_ref[...]
    dm = x.shape[1]
    u = _rms_hat(x) * g_ref[...] * (1.0 + sc_ref[...]) + sh_ref[...]
    ub = u.astype(BF16)
    qkv = _dot(ub, w_ref[...])
    q_ref[...] = (qkv[:, :dm] * SCALE).astype(BF16)
    k_ref[...] = qkv[:, dm:2 * dm]
    v_ref[...] = qkv[:, 2 * dm:]
    lf_ref[...] = _log_sigmoid(_dot(ub, wf_ref[...]) + bf_ref[...])


def _fox_proj_sample(x, g, sc, sh, w_qkv, w_f, b_f):
    n, d = x.shape
    nh = w_f.shape[1]
    return pl.pallas_call(
        _fox_proj_sample_body,
        out_shape=[
            jax.ShapeDtypeStruct((n, d), BF16), jax.ShapeDtypeStruct((n, d), F32),
            jax.ShapeDtypeStruct((n, d), F32), jax.ShapeDtypeStruct((n, nh), F32),
        ],
        compiler_params=pltpu.CompilerParams(vmem_limit_bytes=VMEM_LIMIT_BYTES),
        name="fox_proj_sample",
    )(x, g.reshape(1, d), sc, sh, w_qkv, w_f, b_f.reshape(1, nh))


def _fox_attn_body(qi_ref, kj_ref, q_ref, k_ref, v_ref, dk_ref, dq_ref, o_ref, m_ref, l_ref, acc_ref):
    t = pl.program_id(1)
    qi = qi_ref[t]
    kj = kj_ref[t]
    npair, tq = q_ref.shape[1], q_ref.shape[2]
    tk = k_ref.shape[2]
    lo_half = lax.broadcasted_iota(jnp.int32, (tq, PAIR), 1) < HEAD_DIM
    lo_half_k = lax.broadcasted_iota(jnp.int32, (tk, PAIR), 1) < HEAD_DIM

    @pl.when(kj == 0)
    def _():
        m_ref[...] = jnp.full(m_ref.shape, NEG_INF, F32)
        l_ref[...] = jnp.zeros_like(l_ref)
        acc_ref[...] = jnp.zeros_like(acc_ref)

    def run(masked):
        if masked:
            keep = (lax.broadcasted_iota(jnp.int32, (tq, tk), 1)
                    <= lax.broadcasted_iota(jnp.int32, (tq, tk), 0))

        def pair_body(p, carry):
            qp = q_ref[0, p]
            kp = k_ref[0, p]
            vp = v_ref[0, p]
            pv = jnp.zeros((tq, PAIR), F32)
            alphas = []
            for hh in range(2):
                h = 2 * p + hh
                sel = lo_half if hh == 0 else jnp.logical_not(lo_half)
                sel_k = lo_half_k if hh == 0 else jnp.logical_not(lo_half_k)
                qh = jnp.where(sel, qp, jnp.zeros_like(qp))
                s = lax.dot_general(qh, kp, _NT, preferred_element_type=F32)
                s = s + (dq_ref[0, h][:, 0:1] - dk_ref[0, h])
                if masked:
                    s = jnp.where(keep, s, NEG_INF)
                m_prev = m_ref[h]
                m_new = jnp.maximum(m_prev, jnp.max(s, axis=1, keepdims=True))
                alpha = jnp.exp(m_prev - m_new)
                pr = jnp.exp(s - jnp.tile(m_new, (1, tk // LANES)))
                l_ref[h] = alpha * l_ref[h] + jnp.sum(pr, axis=1, keepdims=True)
                m_ref[h] = m_new
                vh = jnp.where(sel_k, vp, jnp.zeros_like(vp))
                pv = pv + _dot(pr.astype(BF16), vh)
                alphas.append(alpha)
            acc_ref[p] = acc_ref[p] * jnp.where(lo_half, alphas[0], alphas[1]) + pv
            return carry

        lax.fori_loop(0, npair, pair_body, 0)

    @pl.when(kj < qi)
    def _():
        run(False)

    @pl.when(kj == qi)
    def _():
        run(True)
        for p in range(npair):
            l_c = jnp.where(lo_half, l_ref[2 * p], l_ref[2 * p + 1])
            o_ref[0, :, p * PAIR:(p + 1) * PAIR] = (acc_ref[p] / l_c).astype(BF16)


def _fox_attn_prompt(q_pm, k_pm, v_pm, drow, t=512):
    b, npair, s, _ = q_pm.shape
    nh = drow.shape[1]
    nt = s // t
    pairs = [(i, j) for i in range(nt) for j in range(i + 1)]
    qi = jnp.asarray([p[0] for p in pairs], jnp.int32)
    kj = jnp.asarray([p[1] for p in pairs], jnp.int32)
    grid_spec = pltpu.PrefetchScalarGridSpec(
        num_scalar_prefetch=2,
        grid=(b, len(pairs)),
        in_specs=[
            pl.BlockSpec((1, npair, t, PAIR), lambda bi, n, qi, kj: (bi, 0, qi[n], 0)),
            pl.BlockSpec((1, npair, t, PAIR), lambda bi, n, qi, kj: (bi, 0, kj[n], 0)),
            pl.BlockSpec((1, npair, t, PAIR), lambda bi, n, qi, kj: (bi, 0, kj[n], 0)),
            pl.BlockSpec((1, nh, 1, t), lambda bi, n, qi, kj: (bi, 0, 0, kj[n])),
            pl.BlockSpec((1, nh, 1, LANES), lambda bi, n, qi, kj: (bi, 0, 0, qi[n] * (t // LANES))),
        ],
        out_specs=pl.BlockSpec((1, t, npair * PAIR), lambda bi, n, qi, kj: (bi, qi[n], 0)),
        scratch_shapes=[
            pltpu.VMEM((nh, t, LANES), F32), pltpu.VMEM((nh, t, LANES), F32),
            pltpu.VMEM((npair, t, PAIR), F32),
        ],
    )
    return pl.pallas_call(
        _fox_attn_body,
        grid_spec=grid_spec,
        out_shape=jax.ShapeDtypeStruct((b, s, npair * PAIR), BF16),
        compiler_params=_params(2),
        name="fox_attn_prompt",
    )(qi, kj, q_pm, k_pm, v_pm, drow, drow)


def _head_rows(row):
    d = row.shape[1]
    head = lax.broadcasted_iota(jnp.int32, (N_HEADS, d), 0)
    lane = lax.broadcasted_iota(jnp.int32, (N_HEADS, d), 1)
    own = (lane // HEAD_DIM) == head
    return jnp.where(own, jnp.broadcast_to(row, (N_HEADS, d)), 0.0), own


def _fox_decode_body(pt_ref, q_ref, kn_ref, vn_ref, lfn_ref, *refs, pages):
    k_refs = refs[:pages]
    v_refs = refs[pages:2 * pages]
    lf_refs = refs[2 * pages:3 * pages]
    o_ref, qrows_ref, m_ref, l_ref, acc_ref, carry_ref = refs[3 * pages:]
    g = pl.program_id(1)
    page = k_refs[0].shape[1]

    @pl.when(g == 0)
    def _():
        qrows, _ = _head_rows(q_ref[0].astype(F32))
        qrows_ref[...] = qrows.astype(BF16)
        m_ref[...] = jnp.full(m_ref.shape, NEG_INF, F32)
        l_ref[...] = jnp.zeros_like(l_ref)
        acc_ref[...] = jnp.zeros_like(acc_ref)
        carry_ref[...] = jnp.zeros_like(carry_ref)

    qrows = qrows_ref[...]
    lane = lax.broadcasted_iota(jnp.int32, (N_HEADS, page), 1)
    for u in reversed(range(pages)):
        lft = lf_refs[u][0].T
        suf = lft
        shift = 1
        while shift < page:
            suf = suf + jnp.where(lane + shift < page, pltpu.roll(suf, page - shift, 1), 0.0)
            shift *= 2
        carry = carry_ref[...]
        bias = (suf - lft) + carry
        carry_ref[...] = carry + suf[:, 0:1]
        kb = k_refs[u][0].astype(BF16)
        s = lax.dot_general(qrows, kb, _NT, preferred_element_type=F32) + bias
        m_prev = m_ref[...]
        m_new = jnp.maximum(m_prev, jnp.max(s, axis=1, keepdims=True))
        alpha = jnp.exp(m_prev - m_new)
        pr = jnp.exp(s - m_new)
        l_ref[...] = alpha * l_ref[...] + jnp.sum(pr, axis=1, keepdims=True)
        m_ref[...] = m_new
        acc_ref[...] = alpha * acc_ref[...] + _dot(pr.astype(BF16), v_refs[u][0].astype(BF16))

    @pl.when(g == pl.num_programs(1) - 1)
    def _():
        kn = kn_ref[0].astype(BF16).astype(F32)
        vn = vn_ref[0].astype(BF16).astype(F32)
        _, own = _head_rows(kn)
        eye = (lax.broadcasted_iota(jnp.int32, (N_HEADS, N_HEADS), 0)
               == lax.broadcasted_iota(jnp.int32, (N_HEADS, N_HEADS), 1))
        lfn = jnp.sum(jnp.where(eye, jnp.broadcast_to(lfn_ref[0], (N_HEADS, N_HEADS)), 0.0),
                      axis=1, keepdims=True)
        s_new = jnp.sum(qrows.astype(F32) * kn, axis=1, keepdims=True) - lfn
        m_prev = m_ref[...]
        m_new = jnp.maximum(m_prev, s_new)
        alpha = jnp.exp(m_prev - m_new)
        p_new = jnp.exp(s_new - m_new)
        l_fin = alpha * l_ref[...] + p_new
        acc = alpha * acc_ref[...] + p_new * vn
        o_ref[0] = jnp.sum(jnp.where(own, acc / l_fin, 0.0), axis=0, keepdims=True).astype(BF16)


def _fox_attn_sample(q, k_new, v_new, lf_new, cache_k, cache_v, cache_lf, page_table, pages=4):
    n, d = q.shape
    n_pool, page, _ = cache_k.shape
    nh = cache_lf.shape[2]
    n_pages = page_table.shape[1]
    n_groups = n_pages // pages

    def page_map(u):
        return lambda bi, g, pt: (pt[bi, (n_groups - 1 - g) * pages + u], 0, 0)

    seq3 = lambda bi, g, pt: (bi, 0, 0)
    kv_specs = [pl.BlockSpec((1, page, d), page_map(u)) for u in range(pages)]
    lf_specs = [pl.BlockSpec((1, page, nh), page_map(u)) for u in range(pages)]
    grid_spec = pltpu.PrefetchScalarGridSpec(
        num_scalar_prefetch=1,
        grid=(n, n_groups),
        in_specs=[
            pl.BlockSpec((1, 1, d), seq3), pl.BlockSpec((1, 1, d), seq3), pl.BlockSpec((1, 1, d), seq3),
            pl.BlockSpec((1, 1, nh), seq3),
        ] + kv_specs + kv_specs + lf_specs,
        out_specs=pl.BlockSpec((1, 1, d), seq3),
        scratch_shapes=[
            pltpu.VMEM((nh, d), BF16), pltpu.VMEM((nh, 1), F32), pltpu.VMEM((nh, 1), F32),
            pltpu.VMEM((nh, d), F32), pltpu.VMEM((nh, 1), F32),
        ],
    )
    out = pl.pallas_call(
        functools.partial(_fox_decode_body, pages=pages),
        grid_spec=grid_spec,
        out_shape=jax.ShapeDtypeStruct((n, 1, d), BF16),
        compiler_params=_params(2),
        name="fox_attn_sample",
    )(page_table, q.reshape(n, 1, d), k_new.reshape(n, 1, d), v_new.reshape(n, 1, d), lf_new.reshape(n, 1, nh),
      *([cache_k] * pages), *([cache_v] * pages), *([cache_lf] * pages))
    return out.reshape(n, d)


def _post_body(*refs, prompt, chunk):
    if prompt:
        (x_ref, o_ref, wo_ref, g_ref, gt_ref, scf_ref, shf_ref, gtf_ref, wup_ref, wc_ref, bc_ref, wdn_ref,
         h_ref, cv_ref, a_scr, act_scr) = refs
    else:
        (x_ref, o_ref, wo_ref, g_ref, gt_ref, scf_ref, shf_ref, gtf_ref, wup_ref, wc_ref, bc_ref, wdn_ref,
         p0_ref, p1_ref, h_ref, cv_ref, act_scr) = refs
    x = x_ref[0]
    tm = x.shape[0]
    dff = wdn_ref.shape[0]
    g = g_ref[...]
    att = _dot(o_ref[0], wo_ref[...])
    h1 = x + gt_ref[0] * (_rms_hat(att) * g[1:2])
    u = (_rms_hat(h1) * g[2:3] * (1.0 + scf_ref[0]) + shf_ref[0]).astype(BF16)
    if prompt:
        i = pl.program_id(1)
        halo = a_scr.shape[0] - tm

        @pl.when(i == 0)
        def _():
            a_scr[0:halo, :] = jnp.zeros((halo, a_scr.shape[1]), F32)

    for c in range(dff // chunk):
        ys = []
        for base in (0, dff):
            sl = slice(base + c * chunk, base + (c + 1) * chunk)
            a = _dot(u, wup_ref[:, sl])
            if prompt:
                a_scr[halo:halo + tm, sl] = a
                am2 = a_scr[halo - 2:halo - 2 + tm, sl]
                am1 = a_scr[halo - 1:halo - 1 + tm, sl]
            else:
                cv_ref[0, :, sl] = a
                am2 = p0_ref[0, :, sl]
                am1 = p1_ref[0, :, sl]
            ys.append(bc_ref[:, sl] + wc_ref[0:1, sl] * am2 + wc_ref[1:2, sl] * am1 + wc_ref[2:3, sl] * a)
        act_scr[:, c * chunk:(c + 1) * chunk] = (_gelu_tanh(ys[0]) * ys[1]).astype(BF16)
    if prompt:
        @pl.when(i == pl.num_programs(1) - 1)
        def _():
            cv_ref[0] = a_scr[halo + tm - 2:halo + tm, :]

        a_scr[0:halo, :] = a_scr[tm:tm + halo, :]
    ffn = _dot(act_scr[...], wdn_ref[...])
    h_ref[0] = h1 + gtf_ref[0] * (_rms_hat(ffn) * g[3:4])


def _post(x, o, w_out, g4, gt, scf, shf, gtf, w_up, w_conv, b_conv, w_down, prev=None, tm=256, chunk=256):
    b, s, d = x.shape
    dff = w_down.shape[0]
    prompt = prev is None
    tm = min(tm, s)
    halo = 8
    row = lambda bi, i: (bi, i, 0)
    mod_spec = lambda a: pl.BlockSpec((1, tm if a.shape[1] == s and s > 1 else 1, d),
                                      (row if a.shape[1] == s and s > 1 else (lambda bi, i: (bi, 0, 0))))
    in_specs = [
        pl.BlockSpec((1, tm, d), row), pl.BlockSpec((1, tm, d), row), _resident((d, d)), _resident((4, d)),
        mod_spec(gt), mod_spec(scf), mod_spec(shf), mod_spec(gtf),
        _resident((d, 2 * dff)), _resident((3, 2 * dff)), _resident((1, 2 * dff)), _resident((dff, d)),
    ]
    args = [x, o, w_out, g4, gt, scf, shf, gtf, w_up, w_conv, b_conv.reshape(1, 2 * dff), w_down]
    scratch = [pltpu.VMEM((tm, dff), BF16)]
    if prompt:
        cv_shape = jax.ShapeDtypeStruct((b, 2, 2 * dff), F32)
        cv_spec = pl.BlockSpec((1, 2, 2 * dff), lambda bi, i: (bi, 0, 0))
        scratch = [pltpu.VMEM((tm + halo, 2 * dff), F32)] + scratch
    else:
        cv_shape = jax.ShapeDtypeStruct((b, s, 2 * dff), F32)
        cv_spec = pl.BlockSpec((1, tm, 2 * dff), row)
        in_specs += [pl.BlockSpec((1, tm, 2 * dff), row)] * 2
        args += [prev[0], prev[1]]
    return pl.pallas_call(
        functools.partial(_post_body, prompt=prompt, chunk=chunk),
        grid=(b, s // tm),
        in_specs=in_specs,
        out_specs=[pl.BlockSpec((1, tm, d), row), cv_spec],
        out_shape=[jax.ShapeDtypeStruct((b, s, d), F32), cv_shape],
        scratch_shapes=scratch,
        compiler_params=_params(2),
        name="post_prompt" if prompt else "post_sample",
    )(*args)


def _kvq_body(x_ref, gkv_ref, gq_ref, sc_ref, sh_ref, wkv_ref, wq_ref, cos_ref, sin_ref,
              kb_ref, vb_ref, q_ref, kf_ref, vf_ref, *, n_skip):
    i = pl.program_id(1)
    x = x_ref[0]
    dm = x.shape[1]
    xh = _rms_hat(x)
    cos = cos_ref[...]
    sin = sin_ref[...]
    kv = _dot((xh * gkv_ref[...]).astype(BF16), wkv_ref[...])
    k = _rope(kv[:, :dm], cos, sin)
    v = kv[:, dm:]
    uq = (xh * gq_ref[...] * (1.0 + sc_ref[0]) + sh_ref[0]).astype(BF16)
    q = _rope(_dot(uq, wq_ref[...]), cos, sin) * SCALE
    kb_ref[0] = k.astype(BF16)
    vb_ref[0] = v.astype(BF16)
    q_ref[0] = q.astype(BF16)

    @pl.when(i >= n_skip)
    def _():
        kf_ref[0] = k
        vf_ref[0] = v


def _kvq(x, g_kv, g_q, sc, sh, w_kv, w_q, cos, sin, keep_rows, tm=256):
    b, s, d = x.shape
    tm = min(tm, s)
    n_skip = (s - keep_rows) // tm
    row = lambda bi, i: (bi, i, 0)
    per_row = sc.shape[1] == s and s > 1
    mod_spec = pl.BlockSpec((1, tm if per_row else 1, d), row if per_row else (lambda bi, i: (bi, 0, 0)))
    tab_spec = pl.BlockSpec((tm, LANES), lambda bi, i: (i, 0))
    kept = pl.BlockSpec((1, tm, d), lambda bi, i: (bi, jnp.maximum(i - n_skip, 0), 0))
    bf = jax.ShapeDtypeStruct((b, s, d), BF16)
    f32_kept = jax.ShapeDtypeStruct((b, keep_rows, d), F32)
    return pl.pallas_call(
        functools.partial(_kvq_body, n_skip=n_skip),
        grid=(b, s // tm),
        in_specs=[
            pl.BlockSpec((1, tm, d), row), _resident((1, d)), _resident((1, d)), mod_spec, mod_spec,
            _resident((d, 2 * d)), _resident((d, d)), tab_spec, tab_spec,
        ],
        out_specs=[pl.BlockSpec((1, tm, d), row)] * 3 + [kept, kept],
        out_shape=[bf, bf, bf, f32_kept, f32_kept],
        compiler_params=_params(2),
        name="kvq_proj",
    )(x, g_kv.reshape(1, d), g_q.reshape(1, d), sc, sh, w_kv, w_q, cos, sin)


def _dilated_body(*refs, first, last):
    q_ref, kp_ref, kc_ref, vp_ref, vc_ref = refs[:5]
    refs = refs[5:]
    if not first:
        m_in, l_in, acc_in = refs[:3]
        refs = refs[3:]
    if last:
        (o_ref,) = refs
    else:
        m_out, l_out, acc_out = refs
    i = pl.program_id(2)
    tq, dm = q_ref.shape[1], q_ref.shape[2]
    a = lax.broadcasted_iota(jnp.int32, (tq, 2 * tq), 0)
    c = lax.broadcasted_iota(jnp.int32, (tq, 2 * tq), 1)
    keep = (c >= a) & (c <= a + tq) & ((c >= tq) | (i > 0))
    lo_half = lax.broadcasted_iota(jnp.int32, (tq, PAIR), 1) < HEAD_DIM
    lo_half_k = lax.broadcasted_iota(jnp.int32, (2 * tq, PAIR), 1) < HEAD_DIM
    head_lane = lax.broadcasted_iota(jnp.int32, (tq, LANES), 1)
    q = q_ref[0]
    kcat = jnp.concatenate([kp_ref[0], kc_ref[0]], axis=0)
    vcat = jnp.concatenate([vp_ref[0], vc_ref[0]], axis=0)
    m_all = jnp.zeros((tq, LANES), F32)
    l_all = jnp.zeros((tq, LANES), F32)
    for p in range(dm // PAIR):
        sl = slice(p * PAIR, (p + 1) * PAIR)
        qp, kp, vp = q[:, sl], kcat[:, sl], vcat[:, sl]
        pv = jnp.zeros((tq, PAIR), F32)
        alphas, ls = [], []
        for hh in range(2):
            h = 2 * p + hh
            sel = lo_half if hh == 0 else jnp.logical_not(lo_half)
            sel_k = lo_half_k if hh == 0 else jnp.logical_not(lo_half_k)
            qh = jnp.where(sel, qp, jnp.zeros_like(qp))
            s = lax.dot_general(qh, kp, _NT, preferred_element_type=F32)
            s = jnp.where(keep, s, NEG_INF)
            m_cur = jnp.max(s, axis=1, keepdims=True)
            if first:
                m_new = m_cur
                pr = jnp.exp(s - m_new)
                l_new = jnp.sum(pr, axis=1, keepdims=True)
                alpha = None
            else:
                m_prev = m_in[0, :, h:h + 1]
                m_new = jnp.maximum(m_prev, m_cur)
                alpha = jnp.exp(m_prev - m_new)
                pr = jnp.exp(s - m_new)
                l_new = alpha * l_in[0, :, h:h + 1] + jnp.sum(pr, axis=1, keepdims=True)
            vh = jnp.where(sel_k, vp, jnp.zeros_like(vp))
            pv = pv + _dot(pr.astype(BF16), vh)
            alphas.append(alpha)
            ls.append(l_new)
            if not last:
                m_all = jnp.where(head_lane == h, m_new, m_all)
                l_all = jnp.where(head_lane == h, l_new, l_all)
        if first:
            acc = pv
        else:
            acc = acc_in[0, :, sl] * jnp.where(lo_half, alphas[0], alphas[1]) + pv
        if last:
            o_ref[0, :, sl] = (acc / jnp.where(lo_half, ls[0], ls[1])).astype(BF16)
        else:
            acc_out[0, :, sl] = acc
    if not last:
        m_out[0] = m_all
        l_out[0] = l_all


def _dilated_prompt(q, k, v):
    b, s, d = q.shape
    tq = WINDOW_KEYS
    state = None
    out = None
    for n, dil in enumerate(DILATIONS):
        first, last = n == 0, n == len(DILATIONS) - 1
        rows = s // dil
        cur = lambda bi, r, i: (bi, i, r)
        prv = lambda bi, r, i: (bi, jnp.maximum(i - 1, 0), r)
        wide = pl.BlockSpec((1, tq, d), cur)
        narrow = pl.BlockSpec((1, tq, LANES), cur)
        in_specs = [wide, pl.BlockSpec((1, tq, d), prv), wide, pl.BlockSpec((1, tq, d), prv), wide]
        kv_view = lambda a: a.reshape(b, rows, dil * d)
        args = [kv_view(q), kv_view(k), kv_view(k), kv_view(v), kv_view(v)]
        if not first:
            in_specs += [narrow, narrow, wide]
            args += [state[0].reshape(b, rows, dil * LANES), state[1].reshape(b, rows, dil * LANES),
                     state[2].reshape(b, rows, dil * d)]
        if last:
            out_specs = wide
            out_shape = jax.ShapeDtypeStruct((b, rows, dil * d), BF16)
        else:
            out_specs = [narrow, narrow, wide]
            out_shape = [jax.ShapeDtypeStruct((b, rows, dil * LANES), F32)] * 2 + [
                jax.ShapeDtypeStruct((b, rows, dil * d), F32)]
        res = pl.pallas_call(
            functools.partial(_dilated_body, first=first, last=last),
            grid=(b, dil, rows // tq),
            in_specs=in_specs,
            out_specs=out_specs,
            out_shape=out_shape,
            compiler_params=_params(3),
            name=f"dilated_prompt_d{dil}",
        )(*args)
        if last:
            out = res.reshape(b, s, d)
        else:
            state = [res[0].reshape(b, s, LANES), res[1].reshape(b, s, LANES), res[2].reshape(b, s, d)]
    return out


def _dilated_sample_body(q_ref, kn_ref, vn_ref, *refs):
    nb = len(DILATIONS)
    k_refs, v_refs, o_ref = refs[:nb], refs[nb:2 * nb], refs[2 * nb]
    qrows_f32, own = _head_rows(q_ref[0].astype(F32))
    qrows = qrows_f32.astype(BF16)
    kn = kn_ref[0].astype(BF16).astype(F32)
    vn = vn_ref[0].astype(BF16).astype(F32)
    s_self = jnp.sum(qrows_f32 * kn, axis=1, keepdims=True)
    scores = [lax.dot_general(qrows, kr[0].astype(BF16), _NT, preferred_element_type=F32) for kr in k_refs]
    m = s_self
    for s in scores:
        m = jnp.maximum(m, jnp.max(s, axis=1, keepdims=True))
    p_self = jnp.exp(s_self - m)
    l = nb * p_self
    acc = (nb * p_self) * vn
    for s, vr in zip(scores, v_refs):
        pr = jnp.exp(s - m)
        l = l + jnp.sum(pr, axis=1, keepdims=True)
        acc = acc + _dot(pr.astype(BF16), vr[0].astype(BF16))
    o_ref[0] = jnp.sum(jnp.where(own, acc / l, 0.0), axis=0, keepdims=True).astype(BF16)


def _dilated_sample(q, k_new, v_new, cache_k, cache_v):
    n, d = q.shape
    rows = cache_k.shape[1]
    seq3 = lambda bi: (bi, 0, 0)
    specs, k_views, v_views = [], [], []
    for dil in DILATIONS:
        blk = (rows - WINDOW_KEYS * dil) // (WINDOW_KEYS * dil)
        specs.append(pl.BlockSpec((1, WINDOW_KEYS, d), functools.partial(lambda bi, blk: (bi, blk, 0), blk=blk)))
        k_views.append(cache_k.reshape(n, rows // dil, dil * d))
        v_views.append(cache_v.reshape(n, rows // dil, dil * d))
    row_spec = pl.BlockSpec((1, 1, d), seq3)
    out = pl.pallas_call(
        _dilated_sample_body,
        grid=(n,),
        in_specs=[row_spec, row_spec, row_spec] + specs + specs,
        out_specs=row_spec,
        out_shape=jax.ShapeDtypeStruct((n, 1, d), BF16),
        compiler_params=_params(1),
        name="dilated_sample",
    )(q.reshape(n, 1, d), k_new.reshape(n, 1, d), v_new.reshape(n, 1, d), *k_views, *v_views)
    return out.reshape(n, d)


def kernel(x_prompt, x_sample, cache_k_a, cache_v_a, cache_logf_a, page_table, cache_k_b, cache_v_b, state_conv, c_prompt, c_sample, w_mod, b_mod, g_norm, w_in_a, b_f_a, w_out_a, g_kv, w_kv_b, w_q_b, w_out_b, w_up, w_conv, b_conv, w_down):
    bp, sp, d = x_prompt.shape
    ns = x_sample.shape[0]
    depth = w_mod.shape[0]
    assert depth == 2 and x_sample.shape[1] == 1 and d == N_HEADS * HEAD_DIM and cache_k_a.shape[0] == 1
    past_len = page_table.shape[1] * cache_k_a.shape[2]
    wb = cache_k_b.shape[1]
    assert wb == W_MAX and sp >= W_MAX

    pad = (-(bp + ns)) % 8
    c_all = jnp.concatenate([c_prompt, c_sample, jnp.zeros((pad, d), F32)], axis=0)
    mod = _modulation(c_all, w_mod, b_mod).reshape(depth, bp + ns + pad, 6, d)
    mod_p = [[mod[l, :bp, j].reshape(bp, 1, d) for j in range(6)] for l in range(depth)]
    mod_s = [[mod[l, bp:bp + ns, j].reshape(1, ns, d) for j in range(6)] for l in range(depth)]

    w_qkv_a = w_in_a[0, :, :3 * d].astype(BF16)
    w_f_a = w_in_a[0, :, 3 * d:].astype(BF16)
    w_out_a16 = w_out_a[0].astype(BF16)
    w_kv16 = w_kv_b.astype(BF16)
    w_q16 = w_q_b[0].astype(BF16)
    w_out_b16 = w_out_b[0].astype(BF16)
    w_up16 = w_up.astype(BF16)
    w_down16 = w_down.astype(BF16)

    cos_p, sin_p = _rope_tables(jnp.arange(sp))
    cos_s, sin_s = _rope_tables(jnp.full((ns,), past_len, jnp.int32))

    sh, sc, gt, shf, scf, gtf = mod_p[0]
    q_pm, k_pm, v_pm, ka_p, va_p, lfa_p, drow = _fox_proj_prompt(
        x_prompt, g_norm[0, 0], sc, sh, w_qkv_a, w_f_a, b_f_a[0])
    o_p = _fox_attn_prompt(q_pm, k_pm, v_pm, drow)
    h_p, cv_p0 = _post(x_prompt, o_p, w_out_a16, g_norm[0], gt, scf, shf, gtf,
                       w_up16[0], w_conv[0], b_conv[0], w_down16[0])

    sh_s, sc_s, gt_s, shf_s, scf_s, gtf_s = mod_s[0]
    xs = x_sample.reshape(ns, d)
    q_s, ka_s, va_s, lfa_s = _fox_proj_sample(
        xs, g_norm[0, 0], sc_s.reshape(ns, d), sh_s.reshape(ns, d), w_qkv_a, w_f_a, b_f_a[0])
    n_pool = cache_k_a.shape[1]
    o_s = _fox_attn_sample(
        q_s, ka_s, va_s, lfa_s,
        cache_k_a.reshape(n_pool, -1, d), cache_v_a.reshape(n_pool, -1, d),
        cache_logf_a.reshape(n_pool, -1, N_HEADS), page_table)
    prev0 = (state_conv[0][:, 0].reshape(1, ns, -1), state_conv[0][:, 1].reshape(1, ns, -1))
    h_s, a_s0 = _post(xs.reshape(1, ns, d), o_s.reshape(1, ns, d), w_out_a16, g_norm[0], gt_s, scf_s, shf_s, gtf_s,
                      w_up16[0], w_conv[0], b_conv[0], w_down16[0], prev=prev0)

    sh, sc, gt, shf, scf, gtf = mod_p[1]
    kb16_p, vb16_p, qb_p, kb_p, vb_p = _kvq(h_p, g_kv, g_norm[1, 0], sc, sh, w_kv16, w_q16, cos_p, sin_p,
                                            keep_rows=min(W_MAX, sp))
    sh_s, sc_s, gt_s, shf_s, scf_s, gtf_s = mod_s[1]
    _, _, qb_s, kb_s, vb_s = _kvq(h_s, g_kv, g_norm[1, 0], sc_s, sh_s, w_kv16, w_q16, cos_s, sin_s, keep_rows=ns)

    o1_p = _dilated_prompt(qb_p, kb16_p, vb16_p)
    y_p, cv_p1 = _post(h_p, o1_p, w_out_b16, g_norm[1], gt, scf, shf, gtf,
                       w_up16[1], w_conv[1], b_conv[1], w_down16[1])
    o1_s = _dilated_sample(qb_s.reshape(ns, d), kb_s.reshape(ns, d), vb_s.reshape(ns, d),
                           cache_k_b.reshape(ns, wb, d), cache_v_b.reshape(ns, wb, d))
    prev1 = (state_conv[1][:, 0].reshape(1, ns, -1), state_conv[1][:, 1].reshape(1, ns, -1))
    y_s, a_s1 = _post(h_s, o1_s.reshape(1, ns, d), w_out_b16, g_norm[1], gt_s, scf_s, shf_s, gtf_s,
                      w_up16[1], w_conv[1], b_conv[1], w_down16[1], prev=prev1)

    heads = (N_HEADS, HEAD_DIM)
    new_conv_sample = jnp.stack([
        jnp.stack([state_conv[0][:, 1], a_s0.reshape(ns, -1)], axis=1),
        jnp.stack([state_conv[1][:, 1], a_s1.reshape(ns, -1)], axis=1),
    ])
    return (
        y_p,
        y_s.reshape(ns, 1, d),
        ka_p.reshape(1, bp, sp, *heads),
        va_p.reshape(1, bp, sp, *heads),
        lfa_p.reshape(1, bp, sp, N_HEADS),
        ka_s.reshape(1, ns, 1, *heads),
        va_s.reshape(1, ns, 1, *heads),
        lfa_s.reshape(1, ns, 1, N_HEADS),
        kb_p.reshape(bp, -1, *heads),
        vb_p.reshape(bp, -1, *heads),
        kb_s.reshape(ns, 1, *heads),
        vb_s.reshape(ns, 1, *heads),
        jnp.stack([cv_p0, cv_p1]),
        new_conv_sample,
    )
```

```python
import functools

import jax
import jax.numpy as jnp
from jax import lax
from jax.experimental import pallas as pl
from jax.experimental.pallas import tpu as pltpu

F32 = jnp.float32
BF16 = jnp.bfloat16

N_HEADS = 16
HEAD_DIM = 64
DILATIONS = (1, 4, 16)
WINDOW_KEYS = 128
W_MAX = 2048
ROPE_THETA = 10000.0
NORM_EPS = 1e-6
NEG_INF = -1e30
SCALE = HEAD_DIM ** -0.5

LANES = 128
SUBLANES = 8
PAIR = 2 * HEAD_DIM
VMEM_LIMIT_BYTES = 56 * 1024 * 1024

_NT = (((1,), (1,)), ((), ()))
_TN = (((0,), (0,)), ((), ()))


def _params(n_grid_axes):
    return pltpu.CompilerParams(
        dimension_semantics=("arbitrary",) * n_grid_axes,
        vmem_limit_bytes=VMEM_LIMIT_BYTES,
    )


def _resident(shape):
    nd = len(shape)
    return pl.BlockSpec(shape, lambda *_: (0,) * nd, pipeline_mode=pl.Buffered(1))


def _rms_hat(x):
    return x * lax.rsqrt(jnp.mean(x * x, axis=-1, keepdims=True) + NORM_EPS)


def _log_sigmoid(x):
    return jnp.minimum(x, 0.0) - jnp.log1p(jnp.exp(-jnp.abs(x)))


def _gelu_tanh(x):
    return x * (0.5 * (1.0 + jnp.tanh(0.7978845608028654 * (x + 0.044715 * (x * x * x)))))


def _split3(x):
    hi = x.astype(BF16)
    r = x - hi.astype(F32)
    mid = r.astype(BF16)
    lo = (r - mid.astype(F32)).astype(BF16)
    return hi, mid, lo


def _dot(a, b):
    return jnp.dot(a, b, preferred_element_type=F32)


def _rope(x, cos, sin):
    d = x.shape[-1]
    lane = lax.broadcasted_iota(jnp.int32, x.shape, 1)
    first = (lane % HEAD_DIM) < (HEAD_DIM // 2)
    partner = jnp.where(first, pltpu.roll(x, d - HEAD_DIM // 2, 1), pltpu.roll(x, HEAD_DIM // 2, 1))
    reps = d // LANES
    return x * jnp.tile(cos, (1, reps)) + partner * jnp.tile(sin, (1, reps))


def _rope_tables(pos):
    half = HEAD_DIM // 2
    inv = jnp.power(ROPE_THETA, -jnp.arange(half, dtype=F32) * (2.0 / HEAD_DIM))
    ang = pos.astype(F32)[:, None] * inv[None, :]
    cos, sin = jnp.cos(ang), jnp.sin(ang)
    cos_h = jnp.concatenate([cos, cos], axis=-1)
    sin_h = jnp.concatenate([-sin, sin], axis=-1)
    return jnp.tile(cos_h, (1, LANES // HEAD_DIM)), jnp.tile(sin_h, (1, LANES // HEAD_DIM))


def _col_rep(row):
    d = row.shape[1]
    first_row = lax.broadcasted_iota(jnp.int32, (SUBLANES, LANES), 0) == 0
    e0 = jnp.where(first_row, 1.0, 0.0).astype(BF16)
    col = lax.dot_general(jnp.broadcast_to(row, (SUBLANES, d)), e0, _TN, preferred_element_type=F32)
    return col.reshape(d // HEAD_DIM, HEAD_DIM, LANES)


def _cols_to_rows(cols):
    lane = lax.broadcasted_iota(jnp.int32, (HEAD_DIM, LANES), 1)
    packed = jnp.zeros((HEAD_DIM, LANES), F32)
    for j, col in enumerate(cols):
        packed = jnp.where(lane == j, col, packed)
    eye = (lax.broadcasted_iota(jnp.int32, (HEAD_DIM, HEAD_DIM), 0)
           == lax.broadcasted_iota(jnp.int32, (HEAD_DIM, HEAD_DIM), 1))
    eye = jnp.where(eye, 1.0, 0.0).astype(BF16)
    rows = lax.dot_general(packed.astype(BF16), eye, _TN, preferred_element_type=F32)
    return rows[0:len(cols)]


def _mod_body(c_ref, w_ref, b_ref, o_ref):
    c = c_ref[...]
    s = (c * jax.nn.sigmoid(c)).astype(BF16)
    o_ref[0] = _dot(s, w_ref[0].astype(BF16)) + b_ref[0]


def _modulation(c_all, w_mod, b_mod):
    depth, d, n = w_mod.shape
    r = c_all.shape[0]
    tn = 1536
    return pl.pallas_call(
        _mod_body,
        grid=(depth, n // tn),
        in_specs=[
            pl.BlockSpec((r, d), lambda l, j: (0, 0)),
            pl.BlockSpec((1, d, tn), lambda l, j: (l, 0, j)),
            pl.BlockSpec((1, 1, tn), lambda l, j: (l, 0, j)),
        ],
        out_specs=pl.BlockSpec((1, r, tn), lambda l, j: (l, 0, j)),
        out_shape=jax.ShapeDtypeStruct((depth, r, n), F32),
        compiler_params=_params(2),
        name="modulation",
    )(c_all, w_mod, b_mod.reshape(depth, 1, n))


def _fox_proj_prompt_body(x_ref, g_ref, sc_ref, sh_ref, w_ref, wf_ref, wft_ref, bf_ref, bft_ref, tri_ref,
                          q_ref, k_ref, v_ref, kf_ref, vf_ref, lf_ref, d_ref, carry_ref):
    i = pl.program_id(1)
    x = x_ref[0]
    tm, dm = x.shape
    u = _rms_hat(x) * g_ref[...] * (1.0 + sc_ref[0]) + sh_ref[0]
    ub = u.astype(BF16)
    qkv = _dot(ub, w_ref[...])
    k = qkv[:, dm:2 * dm]
    v = qkv[:, 2 * dm:]
    kf_ref[0] = k
    vf_ref[0] = v
    qb = (qkv[:, :dm] * SCALE).astype(BF16)
    kb = k.astype(BF16)
    vb = v.astype(BF16)
    for p in range(dm // PAIR):
        sl = slice(p * PAIR, (p + 1) * PAIR)
        q_ref[0, p] = qb[:, sl]
        k_ref[0, p] = kb[:, sl]
        v_ref[0, p] = vb[:, sl]
    lf_ref[0] = _log_sigmoid(_dot(ub, wf_ref[...]) + bf_ref[...])
    lft = _log_sigmoid(lax.dot_general(wft_ref[...], ub, _NT, preferred_element_type=F32) + bft_ref[...])
    tri = tri_ref[...]
    hi, mid, lo = _split3(lft)
    dloc = _dot(hi, tri) + _dot(mid, tri) + _dot(lo, tri)

    @pl.when(i == 0)
    def _():
        carry_ref[...] = jnp.zeros_like(carry_ref)

    dfull = dloc + carry_ref[...]
    carry_ref[...] = dfull[:, tm - 1:tm]
    for h in range(N_HEADS):
        d_ref[0, h] = dfull[h:h + 1, :]


def _fox_proj_prompt(x, g, sc, sh, w_qkv, w_f, b_f, tm=512):
    b, s, d = x.shape
    npair = d // PAIR
    nh = w_f.shape[1]
    tri = (jnp.arange(tm)[:, None] <= jnp.arange(tm)[None, :]).astype(BF16)
    pm_shape = jax.ShapeDtypeStruct((b, npair, s, PAIR), BF16)
    pm_spec = pl.BlockSpec((1, npair, tm, PAIR), lambda bi, i: (bi, 0, i, 0))
    row_spec = pl.BlockSpec((1, tm, d), lambda bi, i: (bi, i, 0))
    mod_spec = pl.BlockSpec((1, 1, d), lambda bi, i: (bi, 0, 0))
    return pl.pallas_call(
        _fox_proj_prompt_body,
        grid=(b, s // tm),
        in_specs=[
            row_spec, _resident((1, d)), mod_spec, mod_spec,
            _resident((d, 3 * d)), _resident((d, nh)), _resident((nh, d)),
            _resident((1, nh)), _resident((nh, 1)), _resident((tm, tm)),
        ],
        out_specs=[
            pm_spec, pm_spec, pm_spec, row_spec, row_spec,
            pl.BlockSpec((1, tm, nh), lambda bi, i: (bi, i, 0)),
            pl.BlockSpec((1, nh, 1, tm), lambda bi, i: (bi, 0, 0, i)),
        ],
        out_shape=[
            pm_shape, pm_shape, pm_shape,
            jax.ShapeDtypeStruct((b, s, d), F32), jax.ShapeDtypeStruct((b, s, d), F32),
            jax.ShapeDtypeStruct((b, s, nh), F32), jax.ShapeDtypeStruct((b, nh, 1, s), F32),
        ],
        scratch_shapes=[pltpu.VMEM((nh, 1), F32)],
        compiler_params=_params(2),
        name="fox_proj_prompt",
    )(x, g.reshape(1, d), sc, sh, w_qkv, w_f, w_f.T, b_f.reshape(1, nh), b_f.reshape(nh, 1), tri)


def _fox_proj_sample_body(x_ref, g_ref, sc_ref, sh_ref, w_ref, wf_ref, bf_ref, q_ref, k_ref, v_ref, lf_ref):
    x = x_ref[...]
    dm = x.shape[1]
    u = _rms_hat(x) * g_ref[...] * (1.0 + sc_ref[...]) + sh_ref[...]
    ub = u.astype(BF16)
    qkv = _dot(ub, w_ref[...])
    q_ref[...] = (qkv[:, :dm] * SCALE).astype(BF16)
    k_ref[...] = qkv[:, dm:2 * dm]
    v_ref[...] = qkv[:, 2 * dm:]
    lf_ref[...] = _log_sigmoid(_dot(ub, wf_ref[...]) + bf_ref[...])


def _fox_proj_sample(x, g, sc, sh, w_qkv, w_f, b_f):
    n, d = x.shape
    nh = w_f.shape[1]
    return pl.pallas_call(
        _fox_proj_sample_body,
        out_shape=[
            jax.ShapeDtypeStruct((n, d), BF16), jax.ShapeDtypeStruct((n, d), F32),
            jax.ShapeDtypeStruct((n, d), F32), jax.ShapeDtypeStruct((n, nh), F32),
        ],
        compiler_params=pltpu.CompilerParams(vmem_limit_bytes=VMEM_LIMIT_BYTES),
        name="fox_proj_sample",
    )(x, g.reshape(1, d), sc, sh, w_qkv, w_f, b_f.reshape(1, nh))


def _fox_attn_body(qi_ref, kj_ref, q_ref, k_ref, v_ref, dk_ref, dq_ref, o_ref, m_ref, l_ref, acc_ref):
    t = pl.program_id(1)
    qi = qi_ref[t]
    kj = kj_ref[t]
    npair, tq = q_ref.shape[1], q_ref.shape[2]
    tk = k_ref.shape[2]
    lo_half = lax.broadcasted_iota(jnp.int32, (tq, PAIR), 1) < HEAD_DIM
    lo_half_k = lax.broadcasted_iota(jnp.int32, (tk, PAIR), 1) < HEAD_DIM

    @pl.when(kj == 0)
    def _():
        m_ref[...] = jnp.full(m_ref.shape, NEG_INF, F32)
        l_ref[...] = jnp.zeros_like(l_ref)
        acc_ref[...] = jnp.zeros_like(acc_ref)

    def run(masked):
        if masked:
            keep = (lax.broadcasted_iota(jnp.int32, (tq, tk), 1)
                    <= lax.broadcasted_iota(jnp.int32, (tq, tk), 0))

        def pair_body(p, carry):
            qp = q_ref[0, p]
            kp = k_ref[0, p]
            vp = v_ref[0, p]
            pv = jnp.zeros((tq, PAIR), F32)
            alphas = []
            for hh in range(2):
                h = 2 * p + hh
                sel = lo_half if hh == 0 else jnp.logical_not(lo_half)
                sel_k = lo_half_k if hh == 0 else jnp.logical_not(lo_half_k)
                qh = jnp.where(sel, qp, jnp.zeros_like(qp))
                s = lax.dot_general(qh, kp, _NT, preferred_element_type=F32)
                s = s + (dq_ref[0, h][:, 0:1] - dk_ref[0, h])
                if masked:
                    s = jnp.where(keep, s, NEG_INF)
                m_prev = m_ref[h]
                m_new = jnp.maximum(m_prev, jnp.max(s, axis=1, keepdims=True))
                alpha = jnp.exp(m_prev - m_new)
                pr = jnp.exp(s - jnp.tile(m_new, (1, tk // LANES)))
                l_ref[h] = alpha * l_ref[h] + jnp.sum(pr, axis=1, keepdims=True)
                m_ref[h] = m_new
                vh = jnp.where(sel_k, vp, jnp.zeros_like(vp))
                pv = pv + _dot(pr.astype(BF16), vh)
                alphas.append(alpha)
            acc_ref[p] = acc_ref[p] * jnp.where(lo_half, alphas[0], alphas[1]) + pv
            return carry

        lax.fori_loop(0, npair, pair_body, 0)

    @pl.when(kj < qi)
    def _():
        run(False)

    @pl.when(kj == qi)
    def _():
        run(True)
        for p in range(npair):
            l_c = jnp.where(lo_half, l_ref[2 * p], l_ref[2 * p + 1])
            o_ref[0, :, p * PAIR:(p + 1) * PAIR] = (acc_ref[p] / l_c).astype(BF16)


def _fox_attn_prompt(q_pm, k_pm, v_pm, drow, t=512):
    b, npair, s, _ = q_pm.shape
    nh = drow.shape[1]
    nt = s // t
    pairs = [(i, j) for i in range(nt) for j in range(i + 1)]
    qi = jnp.asarray([p[0] for p in pairs], jnp.int32)
    kj = jnp.asarray([p[1] for p in pairs], jnp.int32)
    grid_spec = pltpu.PrefetchScalarGridSpec(
        num_scalar_prefetch=2,
        grid=(b, len(pairs)),
        in_specs=[
            pl.BlockSpec((1, npair, t, PAIR), lambda bi, n, qi, kj: (bi, 0, qi[n], 0)),
            pl.BlockSpec((1, npair, t, PAIR), lambda bi, n, qi, kj: (bi, 0, kj[n], 0)),
            pl.BlockSpec((1, npair, t, PAIR), lambda bi, n, qi, kj: (bi, 0, kj[n], 0)),
            pl.BlockSpec((1, nh, 1, t), lambda bi, n, qi, kj: (bi, 0, 0, kj[n])),
            pl.BlockSpec((1, nh, 1, LANES), lambda bi, n, qi, kj: (bi, 0, 0, qi[n] * (t // LANES))),
        ],
        out_specs=pl.BlockSpec((1, t, npair * PAIR), lambda bi, n, qi, kj: (bi, qi[n], 0)),
        scratch_shapes=[
            pltpu.VMEM((nh, t, LANES), F32), pltpu.VMEM((nh, t, LANES), F32),
            pltpu.VMEM((npair, t, PAIR), F32),
        ],
    )
    return pl.pallas_call(
        _fox_attn_body,
        grid_spec=grid_spec,
        out_shape=jax.ShapeDtypeStruct((b, s, npair * PAIR), BF16),
        compiler_params=_params(2),
        name="fox_attn_prompt",
    )(qi, kj, q_pm, k_pm, v_pm, drow, drow)


def _fox_decode_body(pt_ref, q_ref, kn_ref, vn_ref, lfn_ref, *refs, pages):
    k_refs = refs[:pages]
    v_refs = refs[pages:2 * pages]
    lf_refs = refs[2 * pages:3 * pages]
    o_ref, qb_ref, m_ref, l_ref, acc_ref, carry_ref, s_scr = refs[3 * pages:]
    g = pl.program_id(1)
    nh, page = lf_refs[0].shape[1], lf_refs[0].shape[2]
    lane = lax.broadcasted_iota(jnp.int32, (nh, page), 1)

    @pl.when(g == 0)
    def _():
        qb_ref[...] = _col_rep(q_ref[0])
        m_ref[...] = jnp.full(m_ref.shape, NEG_INF, F32)
        l_ref[...] = jnp.zeros_like(l_ref)
        acc_ref[...] = jnp.zeros_like(acc_ref)
        carry_ref[...] = jnp.zeros_like(carry_ref)

    for u in reversed(range(pages)):
        lft = lf_refs[u][0]
        suf = lft
        shift = 1
        while shift < page:
            suf = suf + jnp.where(lane + shift < page, pltpu.roll(suf, page - shift, 1), 0.0)
            shift *= 2
        carry = carry_ref[...]
        s_scr[:, u * page:(u + 1) * page] = (suf - lft) + carry
        carry_ref[...] = carry + suf[:, 0:1]
    for h in range(nh):
        qh = qb_ref[h]
        for u in range(pages):
            sl = slice(u * page, (u + 1) * page)
            s_scr[h:h + 1, sl] = s_scr[h:h + 1, sl] + jnp.sum(k_refs[u][0, h] * qh, axis=0, keepdims=True)
    s = s_scr[...]
    m_prev = m_ref[...]
    m_new = jnp.maximum(m_prev, jnp.max(s, axis=1, keepdims=True))
    alpha = jnp.exp(m_prev - m_new)
    pr = jnp.exp(s - m_new)
    l_ref[...] = alpha * l_ref[...] + jnp.sum(pr, axis=1, keepdims=True)
    m_ref[...] = m_new
    for h in range(nh):
        acc = acc_ref[h] * alpha[h:h + 1, :]
        for u in range(pages):
            acc = acc + pr[h:h + 1, u * page:(u + 1) * page] * v_refs[u][0, h]
        acc_ref[h] = acc

    @pl.when(g == pl.num_programs(1) - 1)
    def _():
        knb = _col_rep(kn_ref[0].astype(BF16))
        vnb = _col_rep(vn_ref[0].astype(BF16))
        for h in range(nh):
            s_scr[h:h + 1, 0:page] = jnp.sum(knb[h] * qb_ref[h], axis=0, keepdims=True)
        s_new = jnp.where(lane == 0, s_scr[:, 0:page] - lfn_ref[0], NEG_INF)
        m_old = m_ref[...]
        m_fin = jnp.maximum(m_old, jnp.max(s_new, axis=1, keepdims=True))
        a_fin = jnp.exp(m_old - m_fin)
        p_new = jnp.exp(s_new - m_fin)
        l_fin = a_fin * l_ref[...] + jnp.sum(p_new, axis=1, keepdims=True)
        cols = []
        for h in range(nh):
            acc = acc_ref[h] * a_fin[h:h + 1, :] + p_new[h:h + 1, :] * vnb[h]
            cols.append(jnp.sum(acc, axis=1, keepdims=True) / l_fin[h:h + 1, :])
        o_ref[0] = _cols_to_rows(cols).astype(BF16)


def _fox_attn_sample(q, k_new, v_new, lf_new, cache_kt, cache_vt, cache_lft, page_table, pages=8):
    n, d = q.shape
    _, nh, hd, page = cache_kt.shape
    n_groups = page_table.shape[1] // pages

    def page_map(nd, u):
        return lambda bi, g, pt: (pt[bi, (n_groups - 1 - g) * pages + u],) + (0,) * (nd - 1)

    seq3 = lambda bi, g, pt: (bi, 0, 0)
    kv_specs = [pl.BlockSpec((1, nh, hd, page), page_map(4, u)) for u in range(pages)]
    lf_specs = [pl.BlockSpec((1, nh, page), page_map(3, u)) for u in range(pages)]
    grid_spec = pltpu.PrefetchScalarGridSpec(
        num_scalar_prefetch=1,
        grid=(n, n_groups),
        in_specs=[
            pl.BlockSpec((1, 1, d), seq3), pl.BlockSpec((1, 1, d), seq3), pl.BlockSpec((1, 1, d), seq3),
            pl.BlockSpec((1, nh, 1), seq3),
        ] + kv_specs + kv_specs + lf_specs,
        out_specs=pl.BlockSpec((1, nh, hd), seq3),
        scratch_shapes=[
            pltpu.VMEM((nh, hd, LANES), F32), pltpu.VMEM((nh, 1), F32), pltpu.VMEM((nh, 1), F32),
            pltpu.VMEM((nh, hd, LANES), F32), pltpu.VMEM((nh, 1), F32), pltpu.VMEM((nh, pages * page), F32),
        ],
    )
    out = pl.pallas_call(
        functools.partial(_fox_decode_body, pages=pages),
        grid_spec=grid_spec,
        out_shape=jax.ShapeDtypeStruct((n, nh, hd), BF16),
        compiler_params=_params(2),
        name="fox_attn_sample",
    )(page_table, q.reshape(n, 1, d), k_new.reshape(n, 1, d), v_new.reshape(n, 1, d), lf_new.reshape(n, nh, 1),
      *([cache_kt] * pages), *([cache_vt] * pages), *([cache_lft] * pages))
    return out.reshape(n, d)


def _post_body(*refs, prompt, chunk, n_br):
    x_ref = refs[0]
    o_refs = refs[1:1 + n_br]
    refs = refs[1 + n_br:]
    if n_br > 1:
        lse_refs = refs[:n_br]
        e_ref = refs[n_br]
        refs = refs[n_br + 1:]
    wo_ref, g_ref, gt_ref, scf_ref, shf_ref, gtf_ref, wup_ref, wc_ref, bc_ref, wdn_ref = refs[:10]
    refs = refs[10:]
    if prompt:
        h_ref, cv_ref, a_scr, act_scr = refs
    else:
        p0_ref, p1_ref, h_ref, cv_ref, act_scr = refs
    x = x_ref[0]
    tm = x.shape[0]
    dff = wdn_ref.shape[0]
    g = g_ref[...]
    if n_br == 1:
        o = o_refs[0][0]
    else:
        lses = [r[0] for r in lse_refs]
        top = functools.reduce(jnp.maximum, lses)
        es = [jnp.exp(l - top) for l in lses]
        den = functools.reduce(lambda a, b: a + b, es)
        o = None
        for e, o_ref in zip(es, o_refs):
            term = _dot((e / den).astype(BF16), e_ref[...]) * o_ref[0].astype(F32)
            o = term if o is None else o + term
        o = o.astype(BF16)
    att = _dot(o, wo_ref[...])
    h1 = x + gt_ref[0] * (_rms_hat(att) * g[1:2])
    u = (_rms_hat(h1) * g[2:3] * (1.0 + scf_ref[0]) + shf_ref[0]).astype(BF16)
    if prompt:
        i = pl.program_id(1)
        halo = a_scr.shape[0] - tm

        @pl.when(i == 0)
        def _():
            a_scr[0:halo, :] = jnp.zeros((halo, a_scr.shape[1]), F32)

    for c in range(dff // chunk):
        ys = []
        for base in (0, dff):
            sl = slice(base + c * chunk, base + (c + 1) * chunk)
            a = _dot(u, wup_ref[:, sl])
            if prompt:
                a_scr[halo:halo + tm, sl] = a
                am2 = a_scr[halo - 2:halo - 2 + tm, sl]
                am1 = a_scr[halo - 1:halo - 1 + tm, sl]
            else:
                cv_ref[0, :, sl] = a
                am2 = p0_ref[0, :, sl]
                am1 = p1_ref[0, :, sl]
            ys.append(bc_ref[:, sl] + wc_ref[0:1, sl] * am2 + wc_ref[1:2, sl] * am1 + wc_ref[2:3, sl] * a)
        act_scr[:, c * chunk:(c + 1) * chunk] = (_gelu_tanh(ys[0]) * ys[1]).astype(BF16)
    if prompt:
        @pl.when(i == pl.num_programs(1) - 1)
        def _():
            cv_ref[0] = a_scr[halo + tm - 2:halo + tm, :]

        a_scr[0:halo, :] = a_scr[tm:tm + halo, :]
    ffn = _dot(act_scr[...], wdn_ref[...])
    h_ref[0] = h1 + gtf_ref[0] * (_rms_hat(ffn) * g[3:4])


def _post(x, o, w_out, g4, gt, scf, shf, gtf, w_up, w_conv, b_conv, w_down, lse=None, prev=None,
          tm=256, chunk=256):
    b, s, d = x.shape
    dff = w_down.shape[0]
    prompt = prev is None
    o_list = list(o) if isinstance(o, (list, tuple)) else [o]
    n_br = len(o_list)
    tm = min(tm, s)
    halo = SUBLANES
    row = lambda bi, i: (bi, i, 0)

    def mod_spec(a):
        per_row = a.shape[1] == s and s > 1
        return pl.BlockSpec((1, tm if per_row else 1, d), row if per_row else (lambda bi, i: (bi, 0, 0)))

    in_specs = [pl.BlockSpec((1, tm, d), row)] * (1 + n_br)
    args = [x] + o_list
    if n_br > 1:
        head_of_lane = jnp.arange(d)[None, :] // HEAD_DIM
        expand = (jnp.arange(LANES)[:, None] == head_of_lane).astype(BF16)
        in_specs += [pl.BlockSpec((1, tm, LANES), row)] * n_br + [_resident((LANES, d))]
        args += list(lse) + [expand]
    in_specs += [
        _resident((d, d)), _resident((4, d)), mod_spec(gt), mod_spec(scf), mod_spec(shf), mod_spec(gtf),
        _resident((d, 2 * dff)), _resident((3, 2 * dff)), _resident((1, 2 * dff)), _resident((dff, d)),
    ]
    args += [w_out, g4, gt, scf, shf, gtf, w_up, w_conv, b_conv.reshape(1, 2 * dff), w_down]
    scratch = [pltpu.VMEM((tm, dff), BF16)]
    if prompt:
        cv_shape = jax.ShapeDtypeStruct((b, 2, 2 * dff), F32)
        cv_spec = pl.BlockSpec((1, 2, 2 * dff), lambda bi, i: (bi, 0, 0))
        scratch = [pltpu.VMEM((tm + halo, 2 * dff), F32)] + scratch
    else:
        cv_shape = jax.ShapeDtypeStruct((b, s, 2 * dff), F32)
        cv_spec = pl.BlockSpec((1, tm, 2 * dff), row)
        in_specs += [pl.BlockSpec((1, tm, 2 * dff), row)] * 2
        args += [prev[0], prev[1]]
    return pl.pallas_call(
        functools.partial(_post_body, prompt=prompt, chunk=chunk, n_br=n_br),
        grid=(b, s // tm),
        in_specs=in_specs,
        out_specs=[pl.BlockSpec((1, tm, d), row), cv_spec],
        out_shape=[jax.ShapeDtypeStruct((b, s, d), F32), cv_shape],
        scratch_shapes=scratch,
        compiler_params=_params(2),
        name="post_prompt" if prompt else "post_sample",
    )(*args)


def _kvq_body(x_ref, gkv_ref, gq_ref, sc_ref, sh_ref, wkv_ref, wq_ref, cos_ref, sin_ref,
              kb_ref, vb_ref, q_ref, kf_ref, vf_ref, *, n_skip):
    i = pl.program_id(1)
    x = x_ref[0]
    dm = x.shape[1]
    xh = _rms_hat(x)
    cos = cos_ref[...]
    sin = sin_ref[...]
    kv = _dot((xh * gkv_ref[...]).astype(BF16), wkv_ref[...])
    k = _rope(kv[:, :dm], cos, sin)
    v = kv[:, dm:]
    uq = (xh * gq_ref[...] * (1.0 + sc_ref[0]) + sh_ref[0]).astype(BF16)
    q = _rope(_dot(uq, wq_ref[...]), cos, sin) * SCALE
    kb_ref[0] = k.astype(BF16)
    vb_ref[0] = v.astype(BF16)
    q_ref[0] = q.astype(BF16)

    @pl.when(i >= n_skip)
    def _():
        kf_ref[0] = k
        vf_ref[0] = v


def _kvq(x, g_kv, g_q, sc, sh, w_kv, w_q, cos, sin, keep_rows, tm=256):
    b, s, d = x.shape
    tm = min(tm, s)
    n_skip = (s - keep_rows) // tm
    row = lambda bi, i: (bi, i, 0)
    per_row = sc.shape[1] == s and s > 1
    mod_spec = pl.BlockSpec((1, tm if per_row else 1, d), row if per_row else (lambda bi, i: (bi, 0, 0)))
    tab_spec = pl.BlockSpec((tm, LANES), lambda bi, i: (i, 0))
    kept = pl.BlockSpec((1, tm, d), lambda bi, i: (bi, jnp.maximum(i - n_skip, 0), 0))
    bf = jax.ShapeDtypeStruct((b, s, d), BF16)
    f32_kept = jax.ShapeDtypeStruct((b, keep_rows, d), F32)
    return pl.pallas_call(
        functools.partial(_kvq_body, n_skip=n_skip),
        grid=(b, s // tm),
        in_specs=[
            pl.BlockSpec((1, tm, d), row), _resident((1, d)), _resident((1, d)), mod_spec, mod_spec,
            _resident((d, 2 * d)), _resident((d, d)), tab_spec, tab_spec,
        ],
        out_specs=[pl.BlockSpec((1, tm, d), row)] * 3 + [kept, kept],
        out_shape=[bf, bf, bf, f32_kept, f32_kept],
        compiler_params=_params(2),
        name="kvq_proj",
    )(x, g_kv.reshape(1, d), g_q.reshape(1, d), sc, sh, w_kv, w_q, cos, sin)


def _dilated_body(q_ref, kp_ref, kc_ref, vp_ref, vc_ref, o_ref, lse_ref):
    i = pl.program_id(2)
    tq, dm = q_ref.shape[1], q_ref.shape[2]
    a = lax.broadcasted_iota(jnp.int32, (tq, 2 * tq), 0)
    c = lax.broadcasted_iota(jnp.int32, (tq, 2 * tq), 1)
    keep = (c >= a) & (c <= a + tq) & ((c >= tq) | (i > 0))
    lo_half = lax.broadcasted_iota(jnp.int32, (tq, PAIR), 1) < HEAD_DIM
    lo_half_k = lax.broadcasted_iota(jnp.int32, (2 * tq, PAIR), 1) < HEAD_DIM
    head_lane = lax.broadcasted_iota(jnp.int32, (tq, LANES), 1)
    q = q_ref[0]
    kcat = jnp.concatenate([kp_ref[0], kc_ref[0]], axis=0)
    vcat = jnp.concatenate([vp_ref[0], vc_ref[0]], axis=0)
    lse_all = jnp.zeros((tq, LANES), F32)
    for p in range(dm // PAIR):
        sl = slice(p * PAIR, (p + 1) * PAIR)
        qp, kp, vp = q[:, sl], kcat[:, sl], vcat[:, sl]
        pv = jnp.zeros((tq, PAIR), F32)
        ls = []
        for hh in range(2):
            h = 2 * p + hh
            sel = lo_half if hh == 0 else jnp.logical_not(lo_half)
            sel_k = lo_half_k if hh == 0 else jnp.logical_not(lo_half_k)
            qh = jnp.where(sel, qp, jnp.zeros_like(qp))
            s = lax.dot_general(qh, kp, _NT, preferred_element_type=F32)
            s = jnp.where(keep, s, NEG_INF)
            m = jnp.max(s, axis=1, keepdims=True)
            pr = jnp.exp(s - m)
            l = jnp.sum(pr, axis=1, keepdims=True)
            vh = jnp.where(sel_k, vp, jnp.zeros_like(vp))
            pv = pv + _dot(pr.astype(BF16), vh)
            ls.append(l)
            lse_all = jnp.where(head_lane == h, m + jnp.log(l), lse_all)
        o_ref[0, :, sl] = (pv / jnp.where(lo_half, ls[0], ls[1])).astype(BF16)
    lse_ref[0] = lse_all


def _dilated_prompt(q, k, v):
    b, s, d = q.shape
    tq = WINDOW_KEYS
    outs, lses = [], []
    for dil in DILATIONS:
        rows = s // dil
        cur = lambda bi, r, i: (bi, i, r)
        prv = lambda bi, r, i: (bi, jnp.maximum(i - 1, 0), r)
        wide = pl.BlockSpec((1, tq, d), cur)
        wide_prev = pl.BlockSpec((1, tq, d), prv)
        view = lambda arr: arr.reshape(b, rows, dil * d)
        o_g, lse_g = pl.pallas_call(
            _dilated_body,
            grid=(b, dil, rows // tq),
            in_specs=[wide, wide_prev, wide, wide_prev, wide],
            out_specs=[wide, pl.BlockSpec((1, tq, LANES), cur)],
            out_shape=[jax.ShapeDtypeStruct((b, rows, dil * d), BF16),
                       jax.ShapeDtypeStruct((b, rows, dil * LANES), F32)],
            compiler_params=_params(3),
            name=f"dilated_prompt_d{dil}",
        )(view(q), view(k), view(k), view(v), view(v))
        outs.append(o_g.reshape(b, s, d))
        lses.append(lse_g.reshape(b, s, LANES))
    return outs, lses


def _dilated_sample_body(q_ref, kn_ref, vn_ref, k_ref, v_ref, o_ref, qb_ref, knb_ref, vnb_ref):
    j = pl.program_id(1)
    hg, rows = k_ref.shape[1], k_ref.shape[3]

    @pl.when(j == 0)
    def _():
        qb_ref[...] = _col_rep(q_ref[0])
        knb_ref[...] = _col_rep(kn_ref[0].astype(BF16))
        vnb_ref[...] = _col_rep(vn_ref[0].astype(BF16))

    idx = lax.broadcasted_iota(jnp.int32, (1, rows), 1)
    mult = jnp.zeros((1, rows), F32)
    for dil in DILATIONS:
        in_branch = (idx >= rows - WINDOW_KEYS * dil) & (lax.rem(rows - idx, dil) == 0)
        mult = mult + jnp.where(in_branch, 1.0, 0.0)
    valid = mult > 0.0
    n_self = float(len(DILATIONS))
    cols = []
    for h in range(hg):
        qh = qb_ref[j * hg + h]
        s = jnp.sum(k_ref[0, h] * jnp.tile(qh, (1, rows // LANES)), axis=0, keepdims=True)
        s_self = jnp.sum(knb_ref[j * hg + h] * qh, axis=0, keepdims=True)[:, 0:1]
        m = jnp.maximum(jnp.max(jnp.where(valid, s, NEG_INF), axis=1, keepdims=True), s_self)
        pr = jnp.where(valid, mult * jnp.exp(s - m), 0.0)
        p_self = n_self * jnp.exp(s_self - m)
        l = jnp.sum(pr, axis=1, keepdims=True) + p_self
        col = jnp.sum(v_ref[0, h] * pr, axis=1, keepdims=True) + p_self * vnb_ref[j * hg + h][:, 0:1]
        cols.append(col / l)
    o_ref[0] = _cols_to_rows(cols).astype(BF16)


def _dilated_sample(q, k_new, v_new, cache_kt, cache_vt, hg=8):
    n, d = q.shape
    _, nh, hd, rows = cache_kt.shape
    seq3 = lambda bi, j: (bi, 0, 0)
    slab = pl.BlockSpec((1, hg, hd, rows), lambda bi, j: (bi, j, 0, 0))
    row_spec = pl.BlockSpec((1, 1, d), seq3)
    rep = pltpu.VMEM((nh, hd, LANES), F32)
    out = pl.pallas_call(
        _dilated_sample_body,
        grid=(n, nh // hg),
        in_specs=[row_spec, row_spec, row_spec, slab, slab],
        out_specs=pl.BlockSpec((1, hg, hd), lambda bi, j: (bi, j, 0)),
        out_shape=jax.ShapeDtypeStruct((n, nh, hd), BF16),
        scratch_shapes=[rep, rep, rep],
        compiler_params=_params(2),
        name="dilated_sample",
    )(q.reshape(n, 1, d), k_new.reshape(n, 1, d), v_new.reshape(n, 1, d), cache_kt, cache_vt)
    return out.reshape(n, d)


def kernel(x_prompt, x_sample, cache_k_a, cache_v_a, cache_logf_a, page_table, cache_k_b, cache_v_b, state_conv, c_prompt, c_sample, w_mod, b_mod, g_norm, w_in_a, b_f_a, w_out_a, g_kv, w_kv_b, w_q_b, w_out_b, w_up, w_conv, b_conv, w_down):
    bp, sp, d = x_prompt.shape
    ns = x_sample.shape[0]
    depth = w_mod.shape[0]
    assert depth == 2 and x_sample.shape[1] == 1 and d == N_HEADS * HEAD_DIM and cache_k_a.shape[0] == 1
    n_pool, page = cache_k_a.shape[1], cache_k_a.shape[2]
    past_len = page_table.shape[1] * page
    wb = cache_k_b.shape[1]
    assert wb == W_MAX and sp >= W_MAX

    pad = (-(bp + ns)) % SUBLANES
    c_all = jnp.concatenate([c_prompt, c_sample, jnp.zeros((pad, d), F32)], axis=0)
    mod = _modulation(c_all, w_mod, b_mod).reshape(depth, bp + ns + pad, 6, d)
    mod_p = [[mod[l, :bp, j].reshape(bp, 1, d) for j in range(6)] for l in range(depth)]
    mod_s = [[mod[l, bp:bp + ns, j].reshape(1, ns, d) for j in range(6)] for l in range(depth)]

    w_qkv_a = w_in_a[0, :, :3 * d].astype(BF16)
    w_f_a = w_in_a[0, :, 3 * d:].astype(BF16)
    w_out_a16 = w_out_a[0].astype(BF16)
    w_kv16 = w_kv_b.astype(BF16)
    w_q16 = w_q_b[0].astype(BF16)
    w_out_b16 = w_out_b[0].astype(BF16)
    w_up16 = w_up.astype(BF16)
    w_down16 = w_down.astype(BF16)

    cos_p, sin_p = _rope_tables(jnp.arange(sp))
    cos_s, sin_s = _rope_tables(jnp.full((ns,), past_len, jnp.int32))

    cache_kt_a = jnp.transpose(cache_k_a.reshape(n_pool, page, N_HEADS, HEAD_DIM), (0, 2, 3, 1))
    cache_vt_a = jnp.transpose(cache_v_a.reshape(n_pool, page, N_HEADS, HEAD_DIM), (0, 2, 3, 1))
    cache_lft_a = jnp.transpose(cache_logf_a.reshape(n_pool, page, N_HEADS), (0, 2, 1))
    cache_kt_b = jnp.transpose(cache_k_b, (0, 2, 3, 1))
    cache_vt_b = jnp.transpose(cache_v_b, (0, 2, 3, 1))

    sh, sc, gt, shf, scf, gtf = mod_p[0]
    q_pm, k_pm, v_pm, ka_p, va_p, lfa_p, drow = _fox_proj_prompt(
        x_prompt, g_norm[0, 0], sc, sh, w_qkv_a, w_f_a, b_f_a[0])
    o_p = _fox_attn_prompt(q_pm, k_pm, v_pm, drow)
    h_p, cv_p0 = _post(x_prompt, o_p, w_out_a16, g_norm[0], gt, scf, shf, gtf,
                       w_up16[0], w_conv[0], b_conv[0], w_down16[0])

    sh_s, sc_s, gt_s, shf_s, scf_s, gtf_s = mod_s[0]
    xs = x_sample.reshape(ns, d)
    q_s, ka_s, va_s, lfa_s = _fox_proj_sample(
        xs, g_norm[0, 0], sc_s.reshape(ns, d), sh_s.reshape(ns, d), w_qkv_a, w_f_a, b_f_a[0])
    o_s = _fox_attn_sample(q_s, ka_s, va_s, lfa_s, cache_kt_a, cache_vt_a, cache_lft_a, page_table)
    prev0 = (state_conv[0][:, 0].reshape(1, ns, -1), state_conv[0][:, 1].reshape(1, ns, -1))
    h_s, a_s0 = _post(xs.reshape(1, ns, d), o_s.reshape(1, ns, d), w_out_a16, g_norm[0], gt_s, scf_s, shf_s, gtf_s,
                      w_up16[0], w_conv[0], b_conv[0], w_down16[0], prev=prev0)

    sh, sc, gt, shf, scf, gtf = mod_p[1]
    kb16_p, vb16_p, qb_p, kb_p, vb_p = _kvq(h_p, g_kv, g_norm[1, 0], sc, sh, w_kv16, w_q16, cos_p, sin_p,
                                            keep_rows=min(W_MAX, sp))
    sh_s, sc_s, gt_s, shf_s, scf_s, gtf_s = mod_s[1]
    _, _, qb_s, kb_s, vb_s = _kvq(h_s, g_kv, g_norm[1, 0], sc_s, sh_s, w_kv16, w_q16, cos_s, sin_s, keep_rows=ns)

    o1_p, lse_p = _dilated_prompt(qb_p, kb16_p, vb16_p)
    y_p, cv_p1 = _post(h_p, o1_p, w_out_b16, g_norm[1], gt, scf, shf, gtf,
                       w_up16[1], w_conv[1], b_conv[1], w_down16[1], lse=lse_p)
    o1_s = _dilated_sample(qb_s.reshape(ns, d), kb_s.reshape(ns, d), vb_s.reshape(ns, d), cache_kt_b, cache_vt_b)
    prev1 = (state_conv[1][:, 0].reshape(1, ns, -1), state_conv[1][:, 1].reshape(1, ns, -1))
    y_s, a_s1 = _post(h_s, o1_s.reshape(1, ns, d), w_out_b16, g_norm[1], gt_s, scf_s, shf_s, gtf_s,
                      w_up16[1], w_conv[1], b_conv[1], w_down16[1], prev=prev1)

    heads = (N_HEADS, HEAD_DIM)
    new_conv_sample = jnp.stack([
        jnp.stack([state_conv[0][:, 1], a_s0.reshape(ns, -1)], axis=1),
        jnp.stack([state_conv[1][:, 1], a_s1.reshape(ns, -1)], axis=1),
    ])
    return (
        y_p,
        y_s.reshape(ns, 1, d),
        ka_p.reshape(1, bp, sp, *heads),
        va_p.reshape(1, bp, sp, *heads),
        lfa_p.reshape(1, bp, sp, N_HEADS),
        ka_s.reshape(1, ns, 1, *heads),
        va_s.reshape(1, ns, 1, *heads),
        lfa_s.reshape(1, ns, 1, N_HEADS),
        kb_p.reshape(bp, -1, *heads),
        vb_p.reshape(bp, -1, *heads),
        kb_s.reshape(ns, 1, *heads),
        vb_s.reshape(ns, 1, *heads),
        jnp.stack([cv_p0, cv_p1]),
        new_conv_sample,
    )
```

```python
import functools

import jax
import jax.numpy as jnp
from jax import lax
from jax.experimental import pallas as pl
from jax.experimental.pallas import tpu as pltpu

F32 = jnp.float32
BF16 = jnp.bfloat16

N_HEADS = 16
HEAD_DIM = 64
DILATIONS = (1, 4, 16)
WINDOW_KEYS = 128
W_MAX = 2048
ROPE_THETA = 10000.0
NORM_EPS = 1e-6
NEG_INF = -1e30
SCALE = HEAD_DIM ** -0.5
LOG2_E = 1.4426950408889634

LANES = 128
SUBLANES = 8
PAIR = 2 * HEAD_DIM
VMEM_LIMIT_BYTES = 56 * 1024 * 1024
DOWN_PIECES = 2

_NT = (((1,), (1,)), ((), ()))
_TN = (((0,), (0,)), ((), ()))


def _params(n_grid_axes):
    return pltpu.CompilerParams(
        dimension_semantics=("arbitrary",) * n_grid_axes,
        vmem_limit_bytes=VMEM_LIMIT_BYTES,
    )


def _resident(shape):
    nd = len(shape)
    return pl.BlockSpec(shape, lambda *_: (0,) * nd, pipeline_mode=pl.Buffered(1))


def _rms_hat(x):
    return x * lax.rsqrt(jnp.mean(x * x, axis=-1, keepdims=True) + NORM_EPS)


def _log_sigmoid(x):
    return jnp.minimum(x, 0.0) - jnp.log1p(jnp.exp(-jnp.abs(x)))


def _gelu_tanh(x):
    return x * (0.5 * (1.0 + jnp.tanh(0.7978845608028654 * (x + 0.044715 * (x * x * x)))))


def _split3(x):
    hi = x.astype(BF16)
    r = x - hi.astype(F32)
    mid = r.astype(BF16)
    lo = (r - mid.astype(F32)).astype(BF16)
    return hi, mid, lo


def _dot(a, b):
    return jnp.dot(a, b, preferred_element_type=F32)


def _rope(x, cos, sin):
    d = x.shape[-1]
    lane = lax.broadcasted_iota(jnp.int32, x.shape, 1)
    first = (lane % HEAD_DIM) < (HEAD_DIM // 2)
    partner = jnp.where(first, pltpu.roll(x, d - HEAD_DIM // 2, 1), pltpu.roll(x, HEAD_DIM // 2, 1))
    reps = d // LANES
    return x * jnp.tile(cos, (1, reps)) + partner * jnp.tile(sin, (1, reps))


def _rope_tables(pos):
    half = HEAD_DIM // 2
    inv = jnp.power(ROPE_THETA, -jnp.arange(half, dtype=F32) * (2.0 / HEAD_DIM))
    ang = pos.astype(F32)[:, None] * inv[None, :]
    cos, sin = jnp.cos(ang), jnp.sin(ang)
    cos_h = jnp.concatenate([cos, cos], axis=-1)
    sin_h = jnp.concatenate([-sin, sin], axis=-1)
    return jnp.tile(cos_h, (1, LANES // HEAD_DIM)), jnp.tile(sin_h, (1, LANES // HEAD_DIM))


def _col_rep(row):
    d = row.shape[1]
    first_row = lax.broadcasted_iota(jnp.int32, (SUBLANES, LANES), 0) == 0
    e0 = jnp.where(first_row, 1.0, 0.0).astype(BF16)
    col = lax.dot_general(jnp.broadcast_to(row, (SUBLANES, d)), e0, _TN, preferred_element_type=F32)
    return col.reshape(d // HEAD_DIM, HEAD_DIM, LANES)


def _cols_to_rows(cols):
    lane = lax.broadcasted_iota(jnp.int32, (HEAD_DIM, LANES), 1)
    packed = jnp.zeros((HEAD_DIM, LANES), F32)
    for j, col in enumerate(cols):
        packed = jnp.where(lane == j, col, packed)
    eye = (lax.broadcasted_iota(jnp.int32, (HEAD_DIM, HEAD_DIM), 0)
           == lax.broadcasted_iota(jnp.int32, (HEAD_DIM, HEAD_DIM), 1))
    eye = jnp.where(eye, 1.0, 0.0).astype(BF16)
    rows = lax.dot_general(packed.astype(BF16), eye, _TN, preferred_element_type=F32)
    return rows[0:len(cols)]


def _mod_body(c_ref, w_ref, b_ref, o_ref):
    c = c_ref[...]
    s = (c * jax.nn.sigmoid(c)).astype(BF16)
    o_ref[0] = _dot(s, w_ref[0].astype(BF16)) + b_ref[0]


def _modulation(c_all, w_mod, b_mod):
    depth, d, n = w_mod.shape
    r = c_all.shape[0]
    tn = 1536
    return pl.pallas_call(
        _mod_body,
        grid=(depth, n // tn),
        in_specs=[
            pl.BlockSpec((r, d), lambda l, j: (0, 0)),
            pl.BlockSpec((1, d, tn), lambda l, j: (l, 0, j)),
            pl.BlockSpec((1, 1, tn), lambda l, j: (l, 0, j)),
        ],
        out_specs=pl.BlockSpec((1, r, tn), lambda l, j: (l, 0, j)),
        out_shape=jax.ShapeDtypeStruct((depth, r, n), F32),
        compiler_params=_params(2),
        name="modulation",
    )(c_all, w_mod, b_mod.reshape(depth, 1, n))


def _fox_proj_prompt_body(x_ref, g_ref, sc_ref, sh_ref, w_ref, wf_ref, wft_ref, bf_ref, bft_ref, tri_ref,
                          q_ref, k_ref, v_ref, kf_ref, vf_ref, lf_ref, d_ref, carry_ref):
    i = pl.program_id(1)
    x = x_ref[0]
    tm, dm = x.shape
    u = _rms_hat(x) * g_ref[...] * (1.0 + sc_ref[0]) + sh_ref[0]
    ub = u.astype(BF16)
    qkv = _dot(ub, w_ref[...])
    k = qkv[:, dm:2 * dm]
    v = qkv[:, 2 * dm:]
    kf_ref[0] = k
    vf_ref[0] = v
    qb = (qkv[:, :dm] * (SCALE * LOG2_E)).astype(BF16)
    kb = k.astype(BF16)
    vb = v.astype(BF16)
    for p in range(dm // PAIR):
        sl = slice(p * PAIR, (p + 1) * PAIR)
        q_ref[0, p] = qb[:, sl]
        k_ref[0, p] = kb[:, sl]
        v_ref[0, p] = vb[:, sl]
    lf_ref[0] = _log_sigmoid(_dot(ub, wf_ref[...]) + bf_ref[...])
    lft = _log_sigmoid(lax.dot_general(wft_ref[...], ub, _NT, preferred_element_type=F32) + bft_ref[...])
    tri = tri_ref[...]
    hi, mid, lo = _split3(lft)
    dloc = _dot(hi, tri) + _dot(mid, tri) + _dot(lo, tri)

    @pl.when(i == 0)
    def _():
        carry_ref[...] = jnp.zeros_like(carry_ref)

    dfull = dloc + carry_ref[...]
    carry_ref[...] = dfull[:, tm - 1:tm]
    for h in range(N_HEADS):
        d_ref[0, h] = dfull[h:h + 1, :]


def _fox_proj_prompt(x, g, sc, sh, w_qkv, w_f, b_f, tm=512):
    b, s, d = x.shape
    npair = d // PAIR
    nh = w_f.shape[1]
    tri = (jnp.arange(tm)[:, None] <= jnp.arange(tm)[None, :]).astype(BF16)
    pm_shape = jax.ShapeDtypeStruct((b, npair, s, PAIR), BF16)
    pm_spec = pl.BlockSpec((1, npair, tm, PAIR), lambda bi, i: (bi, 0, i, 0))
    row_spec = pl.BlockSpec((1, tm, d), lambda bi, i: (bi, i, 0))
    mod_spec = pl.BlockSpec((1, 1, d), lambda bi, i: (bi, 0, 0))
    return pl.pallas_call(
        _fox_proj_prompt_body,
        grid=(b, s // tm),
        in_specs=[
            row_spec, _resident((1, d)), mod_spec, mod_spec,
            _resident((d, 3 * d)), _resident((d, nh)), _resident((nh, d)),
            _resident((1, nh)), _resident((nh, 1)), _resident((tm, tm)),
        ],
        out_specs=[
            pm_spec, pm_spec, pm_spec, row_spec, row_spec,
            pl.BlockSpec((1, tm, nh), lambda bi, i: (bi, i, 0)),
            pl.BlockSpec((1, nh, 1, tm), lambda bi, i: (bi, 0, 0, i)),
        ],
        out_shape=[
            pm_shape, pm_shape, pm_shape,
            jax.ShapeDtypeStruct((b, s, d), F32), jax.ShapeDtypeStruct((b, s, d), F32),
            jax.ShapeDtypeStruct((b, s, nh), F32), jax.ShapeDtypeStruct((b, nh, 1, s), F32),
        ],
        scratch_shapes=[pltpu.VMEM((nh, 1), F32)],
        compiler_params=_params(2),
        name="fox_proj_prompt",
    )(x, g.reshape(1, d), sc, sh, w_qkv, w_f, w_f.T, b_f.reshape(1, nh), b_f.reshape(nh, 1), tri)


def _fox_proj_sample_body(x_ref, g_ref, sc_ref, sh_ref, w_ref, wf_ref, bf_ref, q_ref, k_ref, v_ref, lf_ref):
    x = x_ref[...]
    dm = x.shape[1]
    u = _rms_hat(x) * g_ref[...] * (1.0 + sc_ref[...]) + sh_ref[...]
    ub = u.astype(BF16)
    qkv = _dot(ub, w_ref[...])
    q_ref[...] = (qkv[:, :dm] * SCALE).astype(BF16)
    k_ref[...] = qkv[:, dm:2 * dm]
    v_ref[...] = qkv[:, 2 * dm:]
    lf_ref[...] = _log_sigmoid(_dot(ub, wf_ref[...]) + bf_ref[...])


def _fox_proj_sample(x, g, sc, sh, w_qkv, w_f, b_f):
    n, d = x.shape
    nh = w_f.shape[1]
    return pl.pallas_call(
        _fox_proj_sample_body,
        out_shape=[
            jax.ShapeDtypeStruct((n, d), BF16), jax.ShapeDtypeStruct((n, d), F32),
            jax.ShapeDtypeStruct((n, d), F32), jax.ShapeDtypeStruct((n, nh), F32),
        ],
        compiler_params=pltpu.CompilerParams(vmem_limit_bytes=VMEM_LIMIT_BYTES),
        name="fox_proj_sample",
    )(x, g.reshape(1, d), sc, sh, w_qkv, w_f, b_f.reshape(1, nh))


def _fox_attn_body(qi_ref, kj_ref, q_ref, k_ref, v_ref, dk_ref, dq_ref, o_ref, m_ref, l_ref, acc_ref):
    t = pl.program_id(1)
    qi = qi_ref[t]
    kj = kj_ref[t]
    npair, tq = q_ref.shape[1], q_ref.shape[2]
    tk = k_ref.shape[2]
    lo_half = lax.broadcasted_iota(jnp.int32, (tq, PAIR), 1) < HEAD_DIM
    lo_half_k = lax.broadcasted_iota(jnp.int32, (tk, PAIR), 1) < HEAD_DIM

    @pl.when(kj == 0)
    def _():
        m_ref[...] = jnp.full(m_ref.shape, NEG_INF, F32)
        l_ref[...] = jnp.zeros_like(l_ref)
        acc_ref[...] = jnp.zeros_like(acc_ref)

    def run(masked):
        if masked:
            keep = (lax.broadcasted_iota(jnp.int32, (tq, tk), 1)
                    <= lax.broadcasted_iota(jnp.int32, (tq, tk), 0))

        def pair_body(p, carry):
            qp = q_ref[0, p]
            kp = k_ref[0, p]
            vp = v_ref[0, p]
            pv = jnp.zeros((tq, PAIR), F32)
            alphas = []
            for hh in range(2):
                h = 2 * p + hh
                sel = lo_half if hh == 0 else jnp.logical_not(lo_half)
                sel_k = lo_half_k if hh == 0 else jnp.logical_not(lo_half_k)
                qh = jnp.where(sel, qp, jnp.zeros_like(qp))
                s = lax.dot_general(qh, kp, _NT, preferred_element_type=F32)
                s = s + (dq_ref[0, h][:, 0:1] - dk_ref[0, h]) * LOG2_E
                if masked:
                    s = jnp.where(keep, s, NEG_INF)
                m_prev = m_ref[h]
                m_new = jnp.maximum(m_prev, jnp.max(s, axis=1, keepdims=True))
                alpha = jnp.exp2(m_prev - m_new)
                pr = jnp.exp2(s - jnp.tile(m_new, (1, tk // LANES)))
                l_ref[h] = alpha * l_ref[h] + jnp.sum(pr, axis=1, keepdims=True)
                m_ref[h] = m_new
                vh = jnp.where(sel_k, vp, jnp.zeros_like(vp))
                pv = pv + _dot(pr.astype(BF16), vh)
                alphas.append(alpha)
            acc_ref[p] = acc_ref[p] * jnp.where(lo_half, alphas[0], alphas[1]) + pv
            return carry

        lax.fori_loop(0, npair, pair_body, 0, unroll=2)

    @pl.when(kj < qi)
    def _():
        run(False)

    @pl.when(kj == qi)
    def _():
        run(True)
        for p in range(npair):
            l_c = jnp.where(lo_half, l_ref[2 * p], l_ref[2 * p + 1])
            o_ref[0, :, p * PAIR:(p + 1) * PAIR] = (acc_ref[p] / l_c).astype(BF16)


def _fox_attn_prompt(q_pm, k_pm, v_pm, drow, t=512):
    b, npair, s, _ = q_pm.shape
    nh = drow.shape[1]
    nt = s // t
    pairs = [(i, j) for i in range(nt) for j in range(i + 1)]
    qi = jnp.asarray([p[0] for p in pairs], jnp.int32)
    kj = jnp.asarray([p[1] for p in pairs], jnp.int32)
    grid_spec = pltpu.PrefetchScalarGridSpec(
        num_scalar_prefetch=2,
        grid=(b, len(pairs)),
        in_specs=[
            pl.BlockSpec((1, npair, t, PAIR), lambda bi, n, qi, kj: (bi, 0, qi[n], 0)),
            pl.BlockSpec((1, npair, t, PAIR), lambda bi, n, qi, kj: (bi, 0, kj[n], 0)),
            pl.BlockSpec((1, npair, t, PAIR), lambda bi, n, qi, kj: (bi, 0, kj[n], 0)),
            pl.BlockSpec((1, nh, 1, t), lambda bi, n, qi, kj: (bi, 0, 0, kj[n])),
            pl.BlockSpec((1, nh, 1, LANES), lambda bi, n, qi, kj: (bi, 0, 0, qi[n] * (t // LANES))),
        ],
        out_specs=pl.BlockSpec((1, t, npair * PAIR), lambda bi, n, qi, kj: (bi, qi[n], 0)),
        scratch_shapes=[
            pltpu.VMEM((nh, t, LANES), F32), pltpu.VMEM((nh, t, LANES), F32),
            pltpu.VMEM((npair, t, PAIR), F32),
        ],
    )
    return pl.pallas_call(
        _fox_attn_body,
        grid_spec=grid_spec,
        out_shape=jax.ShapeDtypeStruct((b, s, npair * PAIR), BF16),
        compiler_params=_params(2),
        name="fox_attn_prompt",
    )(qi, kj, q_pm, k_pm, v_pm, drow, drow)


def _fox_decode_body(pt_ref, q_ref, kn_ref, vn_ref, lfn_ref, *refs, pages):
    k_refs = refs[:pages]
    v_refs = refs[pages:2 * pages]
    lf_refs = refs[2 * pages:3 * pages]
    o_ref, qb_ref, m_ref, l_ref, acc_ref, carry_ref, s_scr = refs[3 * pages:]
    g = pl.program_id(1)
    nh, page = lf_refs[0].shape[1], lf_refs[0].shape[2]
    lane = lax.broadcasted_iota(jnp.int32, (nh, page), 1)

    @pl.when(g == 0)
    def _():
        qb_ref[...] = _col_rep(q_ref[0])
        m_ref[...] = jnp.full(m_ref.shape, NEG_INF, F32)
        l_ref[...] = jnp.zeros_like(l_ref)
        acc_ref[...] = jnp.zeros_like(acc_ref)
        carry_ref[...] = jnp.zeros_like(carry_ref)

    for u in reversed(range(pages)):
        lft = lf_refs[u][0]
        suf = lft
        shift = 1
        while shift < page:
            suf = suf + jnp.where(lane + shift < page, pltpu.roll(suf, page - shift, 1), 0.0)
            shift *= 2
        carry = carry_ref[...]
        s_scr[:, u * page:(u + 1) * page] = (suf - lft) + carry
        carry_ref[...] = carry + suf[:, 0:1]
    for h in range(nh):
        qh = qb_ref[h]
        for u in range(pages):
            sl = slice(u * page, (u + 1) * page)
            s_scr[h:h + 1, sl] = s_scr[h:h + 1, sl] + jnp.sum(k_refs[u][0, h] * qh, axis=0, keepdims=True)
    s = s_scr[...]
    m_prev = m_ref[...]
    m_new = jnp.maximum(m_prev, jnp.max(s, axis=1, keepdims=True))
    alpha = jnp.exp(m_prev - m_new)
    pr = jnp.exp(s - m_new)
    l_ref[...] = alpha * l_ref[...] + jnp.sum(pr, axis=1, keepdims=True)
    m_ref[...] = m_new
    for h in range(nh):
        acc = acc_ref[h] * alpha[h:h + 1, :]
        for u in range(pages):
            acc = acc + pr[h:h + 1, u * page:(u + 1) * page] * v_refs[u][0, h]
        acc_ref[h] = acc

    @pl.when(g == pl.num_programs(1) - 1)
    def _():
        knb = _col_rep(kn_ref[0].astype(BF16))
        vnb = _col_rep(vn_ref[0].astype(BF16))
        for h in range(nh):
            s_scr[h:h + 1, 0:page] = jnp.sum(knb[h] * qb_ref[h], axis=0, keepdims=True)
        s_new = jnp.where(lane == 0, s_scr[:, 0:page] - lfn_ref[0], NEG_INF)
        m_old = m_ref[...]
        m_fin = jnp.maximum(m_old, jnp.max(s_new, axis=1, keepdims=True))
        a_fin = jnp.exp(m_old - m_fin)
        p_new = jnp.exp(s_new - m_fin)
        l_fin = a_fin * l_ref[...] + jnp.sum(p_new, axis=1, keepdims=True)
        cols = []
        for h in range(nh):
            acc = acc_ref[h] * a_fin[h:h + 1, :] + p_new[h:h + 1, :] * vnb[h]
            cols.append(jnp.sum(acc, axis=1, keepdims=True) / l_fin[h:h + 1, :])
        o_ref[0] = _cols_to_rows(cols).astype(BF16)


def _fox_attn_sample(q, k_new, v_new, lf_new, cache_kt, cache_vt, cache_lft, page_table, pages=16):
    n, d = q.shape
    _, nh, hd, page = cache_kt.shape
    n_groups = page_table.shape[1] // pages

    def page_map(nd, u):
        return lambda bi, g, pt: (pt[bi, (n_groups - 1 - g) * pages + u],) + (0,) * (nd - 1)

    seq3 = lambda bi, g, pt: (bi, 0, 0)
    kv_specs = [pl.BlockSpec((1, nh, hd, page), page_map(4, u)) for u in range(pages)]
    lf_specs = [pl.BlockSpec((1, nh, page), page_map(3, u)) for u in range(pages)]
    grid_spec = pltpu.PrefetchScalarGridSpec(
        num_scalar_prefetch=1,
        grid=(n, n_groups),
        in_specs=[
            pl.BlockSpec((1, 1, d), seq3), pl.BlockSpec((1, 1, d), seq3), pl.BlockSpec((1, 1, d), seq3),
            pl.BlockSpec((1, nh, 1), seq3),
        ] + kv_specs + kv_specs + lf_specs,
        out_specs=pl.BlockSpec((1, nh, hd), seq3),
        scratch_shapes=[
            pltpu.VMEM((nh, hd, LANES), F32), pltpu.VMEM((nh, 1), F32), pltpu.VMEM((nh, 1), F32),
            pltpu.VMEM((nh, hd, LANES), F32), pltpu.VMEM((nh, 1), F32), pltpu.VMEM((nh, pages * page), F32),
        ],
    )
    out = pl.pallas_call(
        functools.partial(_fox_decode_body, pages=pages),
        grid_spec=grid_spec,
        out_shape=jax.ShapeDtypeStruct((n, nh, hd), BF16),
        compiler_params=_params(2),
        name="fox_attn_sample",
    )(page_table, q.reshape(n, 1, d), k_new.reshape(n, 1, d), v_new.reshape(n, 1, d), lf_new.reshape(n, nh, 1),
      *([cache_kt] * pages), *([cache_vt] * pages), *([cache_lft] * pages))
    return out.reshape(n, d)


def _post_body(*refs, prompt, chunk, n_br):
    x_ref = refs[0]
    o_refs = refs[1:1 + n_br]
    refs = refs[1 + n_br:]
    if n_br > 1:
        lse_refs = refs[:n_br]
        e_ref = refs[n_br]
        refs = refs[n_br + 1:]
    wo_ref, g_ref, gt_ref, scf_ref, shf_ref, gtf_ref, wup_ref, wc_ref, bc_ref, wdn_ref = refs[:10]
    refs = refs[10:]
    if prompt:
        h_ref, cv_ref, a_scr, act_scr = refs[:4]
    else:
        p0_ref, p1_ref, h_ref, cv_ref, act_scr = refs[:5]
    x = x_ref[0]
    tm, dm = x.shape
    dff = wdn_ref.shape[0]
    g = g_ref[...]
    if n_br == 1:
        o = o_refs[0][0]
    else:
        nat_scr, lse_scr = refs[-2:]
        nslab = dm // LANES
        outs, lses = [], []
        ci = 0
        for dil, o_ref, lse_ref in zip(DILATIONS, o_refs, lse_refs):
            if dil == 1:
                outs.append(o_ref[0, 0].astype(F32))
                lses.append(lse_ref[0, 0])
                continue
            for r in range(dil):
                lse_scr[ci, pl.ds(r, tm // dil, stride=dil), :] = lse_ref[0, r]
                for j in range(nslab):
                    rows = o_ref[0, r, :, j * LANES:(j + 1) * LANES].astype(F32)
                    nat_scr[ci, j, pl.ds(r, tm // dil, stride=dil), :] = rows
            outs.append(jnp.concatenate([nat_scr[ci, j] for j in range(nslab)], axis=1))
            lses.append(lse_scr[ci])
            ci += 1
        top = functools.reduce(jnp.maximum, lses)
        es = [jnp.exp(l - top) for l in lses]
        den = functools.reduce(lambda a, b: a + b, es)
        o = None
        for e, o_g in zip(es, outs):
            term = _dot((e / den).astype(BF16), e_ref[...]) * o_g
            o = term if o is None else o + term
        o = o.astype(BF16)
    att = _dot(o, wo_ref[...])
    h1 = x + gt_ref[0] * (_rms_hat(att) * g[1:2])
    u = (_rms_hat(h1) * g[2:3] * (1.0 + scf_ref[0]) + shf_ref[0]).astype(BF16)
    if prompt:
        i = pl.program_id(1)
        halo = a_scr.shape[0] - tm

        @pl.when(i == 0)
        def _():
            a_scr[0:halo, :] = jnp.zeros((halo, a_scr.shape[1]), F32)

    n_chunks = dff // chunk
    flush_at = [((k + 1) * n_chunks) // DOWN_PIECES - 1 for k in range(DOWN_PIECES)]
    ffn = None
    start = 0
    for c in range(n_chunks):
        ys = []
        for base in (0, dff):
            sl = slice(base + c * chunk, base + (c + 1) * chunk)
            a = _dot(u, wup_ref[:, sl])
            if prompt:
                a_scr[halo:halo + tm, sl] = a
                am2 = a_scr[halo - 2:halo - 2 + tm, sl]
                am1 = a_scr[halo - 1:halo - 1 + tm, sl]
            else:
                cv_ref[0, :, sl] = a
                am2 = p0_ref[0, :, sl]
                am1 = p1_ref[0, :, sl]
            ys.append(bc_ref[:, sl] + wc_ref[0:1, sl] * am2 + wc_ref[1:2, sl] * am1 + wc_ref[2:3, sl] * a)
        act_scr[:, c * chunk:(c + 1) * chunk] = (_gelu_tanh(ys[0]) * ys[1]).astype(BF16)
        if c in flush_at:
            stop = (c + 1) * chunk
            part = _dot(act_scr[:, start:stop], wdn_ref[start:stop, :])
            ffn = part if ffn is None else ffn + part
            start = stop
    if prompt:
        @pl.when(i == pl.num_programs(1) - 1)
        def _():
            cv_ref[0] = a_scr[halo + tm - 2:halo + tm, :]

        a_scr[0:halo, :] = a_scr[tm:tm + halo, :]
    h_ref[0] = h1 + gtf_ref[0] * (_rms_hat(ffn) * g[3:4])


def _post(x, o, w_out, g4, gt, scf, shf, gtf, w_up, w_conv, b_conv, w_down, lse=None, prev=None,
          tm=256, chunk=256):
    b, s, d = x.shape
    dff = w_down.shape[0]
    prompt = prev is None
    o_list = list(o) if isinstance(o, (list, tuple)) else [o]
    n_br = len(o_list)
    tm = min(tm, s)
    halo = SUBLANES
    row = lambda bi, i: (bi, i, 0)

    def mod_spec(a):
        per_row = a.shape[1] == s and s > 1
        return pl.BlockSpec((1, tm if per_row else 1, d), row if per_row else (lambda bi, i: (bi, 0, 0)))

    in_specs = [pl.BlockSpec((1, tm, d), row)]
    args = [x] + o_list
    scratch_mix = []
    if n_br == 1:
        in_specs += [pl.BlockSpec((1, tm, d), row)]
    else:
        head_of_lane = jnp.arange(d)[None, :] // HEAD_DIM
        expand = (jnp.arange(LANES)[:, None] == head_of_lane).astype(BF16)
        cls = lambda bi, i: (bi, 0, i, 0)
        in_specs += [pl.BlockSpec((1, dil, tm // dil, d), cls) for dil in DILATIONS]
        in_specs += [pl.BlockSpec((1, dil, tm // dil, LANES), cls) for dil in DILATIONS]
        in_specs += [_resident((LANES, d))]
        args += list(lse) + [expand]
        n_cls = sum(1 for dil in DILATIONS if dil > 1)
        scratch_mix = [pltpu.VMEM((n_cls, d // LANES, tm, LANES), F32), pltpu.VMEM((n_cls, tm, LANES), F32)]
    in_specs += [
        _resident((d, d)), _resident((4, d)), mod_spec(gt), mod_spec(scf), mod_spec(shf), mod_spec(gtf),
        _resident((d, 2 * dff)), _resident((3, 2 * dff)), _resident((1, 2 * dff)), _resident((dff, d)),
    ]
    args += [w_out, g4, gt, scf, shf, gtf, w_up, w_conv, b_conv.reshape(1, 2 * dff), w_down]
    scratch = [pltpu.VMEM((tm, dff), BF16)]
    if prompt:
        cv_shape = jax.ShapeDtypeStruct((b, 2, 2 * dff), F32)
        cv_spec = pl.BlockSpec((1, 2, 2 * dff), lambda bi, i: (bi, 0, 0))
        scratch = [pltpu.VMEM((tm + halo, 2 * dff), F32)] + scratch
    else:
        cv_shape = jax.ShapeDtypeStruct((b, s, 2 * dff), F32)
        cv_spec = pl.BlockSpec((1, tm, 2 * dff), row)
        in_specs += [pl.BlockSpec((1, tm, 2 * dff), row)] * 2
        args += [prev[0], prev[1]]
    return pl.pallas_call(
        functools.partial(_post_body, prompt=prompt, chunk=chunk, n_br=n_br),
        grid=(b, s // tm),
        in_specs=in_specs,
        out_specs=[pl.BlockSpec((1, tm, d), row), cv_spec],
        out_shape=[jax.ShapeDtypeStruct((b, s, d), F32), cv_shape],
        scratch_shapes=scratch + scratch_mix,
        compiler_params=_params(2),
        name="post_prompt" if prompt else "post_sample",
    )(*args)


def _kvq_body(x_ref, gkv_ref, gq_ref, sc_ref, sh_ref, wkv_ref, wq_ref, cos_ref, sin_ref, *refs, n_skip, dils):
    nd = len(dils)
    q_refs, k_refs, v_refs = refs[:nd], refs[nd:2 * nd], refs[2 * nd:3 * nd]
    kf_ref, vf_ref = refs[3 * nd:3 * nd + 2]
    i = pl.program_id(1)
    x = x_ref[0]
    tm, dm = x.shape
    xh = _rms_hat(x)
    cos = cos_ref[...]
    sin = sin_ref[...]
    kv = _dot((xh * gkv_ref[...]).astype(BF16), wkv_ref[...])
    k = _rope(kv[:, :dm], cos, sin)
    v = kv[:, dm:]
    uq = (xh * gq_ref[...] * (1.0 + sc_ref[0]) + sh_ref[0]).astype(BF16)
    q = _rope(_dot(uq, wq_ref[...]), cos, sin) * SCALE

    @pl.when(i >= n_skip)
    def _():
        kf_ref[0] = k
        vf_ref[0] = v

    nslab = dm // LANES
    for val, outs in ((q, q_refs), (k, k_refs), (v, v_refs)):
        outs[0][0, 0] = val.astype(BF16)
        if nd > 1:
            stages = refs[3 * nd + 2:]
            for j in range(nslab):
                stages[0][0, j] = val[:, j * LANES:(j + 1) * LANES]
            for lvl in range(1, nd):
                step = dils[lvl] // dils[lvl - 1]
                n_rows = tm // dils[lvl]
                for r_prev in range(dils[lvl - 1]):
                    for r_sub in range(step):
                        r = r_prev + dils[lvl - 1] * r_sub
                        for j in range(nslab):
                            rows = stages[lvl - 1][r_prev, j, pl.ds(r_sub, n_rows, stride=step), :]
                            outs[lvl][0, r, :, j * LANES:(j + 1) * LANES] = rows.astype(BF16)
                            if lvl + 1 < nd:
                                stages[lvl][r, j] = rows


def _kvq(x, g_kv, g_q, sc, sh, w_kv, w_q, cos, sin, keep_rows, dils=(1,), tm=256):
    b, s, d = x.shape
    tm = min(tm, s)
    assert dils[0] == 1 and all(nxt % prv == 0 for prv, nxt in zip(dils, dils[1:]))
    n_skip = (s - keep_rows) // tm
    row = lambda bi, i: (bi, i, 0)
    per_row = sc.shape[1] == s and s > 1
    mod_spec = pl.BlockSpec((1, tm if per_row else 1, d), row if per_row else (lambda bi, i: (bi, 0, 0)))
    tab_spec = pl.BlockSpec((tm, LANES), lambda bi, i: (i, 0))
    kept = pl.BlockSpec((1, tm, d), lambda bi, i: (bi, jnp.maximum(i - n_skip, 0), 0))
    f32_kept = jax.ShapeDtypeStruct((b, keep_rows, d), F32)
    cls_specs = [pl.BlockSpec((1, dil, tm // dil, d), lambda bi, i: (bi, 0, i, 0)) for dil in dils]
    cls_shapes = [jax.ShapeDtypeStruct((b, dil, s // dil, d), BF16) for dil in dils]
    res = pl.pallas_call(
        functools.partial(_kvq_body, n_skip=n_skip, dils=tuple(dils)),
        grid=(b, s // tm),
        in_specs=[
            pl.BlockSpec((1, tm, d), row), _resident((1, d)), _resident((1, d)), mod_spec, mod_spec,
            _resident((d, 2 * d)), _resident((d, d)), tab_spec, tab_spec,
        ],
        out_specs=cls_specs * 3 + [kept, kept],
        out_shape=cls_shapes * 3 + [f32_kept, f32_kept],
        scratch_shapes=[pltpu.VMEM((dil, d // LANES, tm // dil, LANES), F32) for dil in dils[:-1]],
        compiler_params=_params(2),
        name="kvq_proj",
    )(x, g_kv.reshape(1, d), g_q.reshape(1, d), sc, sh, w_kv, w_q, cos, sin)
    nd = len(dils)
    return res[:nd], res[nd:2 * nd], res[2 * nd:3 * nd], res[3 * nd], res[3 * nd + 1]


def _dilated_body(q_ref, kp_ref, kc_ref, vp_ref, vc_ref, o_ref, lse_ref):
    i = pl.program_id(2)
    tq, dm = q_ref.shape
    a = lax.broadcasted_iota(jnp.int32, (tq, 2 * tq), 0)
    c = lax.broadcasted_iota(jnp.int32, (tq, 2 * tq), 1)
    keep = (c >= a) & (c <= a + tq) & ((c >= tq) | (i > 0))
    keep2 = jnp.concatenate([keep, keep], axis=0)
    lo_half = lax.broadcasted_iota(jnp.int32, (tq, PAIR), 1) < HEAD_DIM
    head_lane = lax.broadcasted_iota(jnp.int32, (tq, LANES), 1)
    q = q_ref[...]
    kcat = jnp.concatenate([kp_ref[...], kc_ref[...]], axis=0)
    vcat = jnp.concatenate([vp_ref[...], vc_ref[...]], axis=0)
    lse_all = jnp.zeros((tq, LANES), F32)
    for p in range(dm // PAIR):
        sl = slice(p * PAIR, (p + 1) * PAIR)
        qp, kp, vp = q[:, sl], kcat[:, sl], vcat[:, sl]
        zero = jnp.zeros_like(qp)
        q2 = jnp.concatenate([jnp.where(lo_half, qp, zero), jnp.where(lo_half, zero, qp)], axis=0)
        s = lax.dot_general(q2, kp, _NT, preferred_element_type=F32)
        s = jnp.where(keep2, s, NEG_INF)
        m = jnp.max(s, axis=1, keepdims=True)
        pr = jnp.exp(s - m)
        l = jnp.sum(pr, axis=1, keepdims=True)
        pv = _dot(pr.astype(BF16), vp) / l
        lse = m + jnp.log(l)
        o_ref[:, sl] = jnp.where(lo_half, pv[:tq], pv[tq:]).astype(BF16)
        lse_all = jnp.where(head_lane == 2 * p, lse[:tq], lse_all)
        lse_all = jnp.where(head_lane == 2 * p + 1, lse[tq:], lse_all)
    lse_ref[...] = lse_all


def _dilated_prompt(qs, ks, vs):
    tq = WINDOW_KEYS
    outs, lses = [], []
    for dil, q, k, v in zip(DILATIONS, qs, ks, vs):
        b, _, rows, d = q.shape
        cur = lambda bi, r, i: (bi, r, i, 0)
        prv = lambda bi, r, i: (bi, r, jnp.maximum(i - 1, 0), 0)
        wide = pl.BlockSpec((None, None, tq, d), cur)
        wide_prev = pl.BlockSpec((None, None, tq, d), prv)
        o_g, lse_g = pl.pallas_call(
            _dilated_body,
            grid=(b, dil, rows // tq),
            in_specs=[wide, wide_prev, wide, wide_prev, wide],
            out_specs=[wide, pl.BlockSpec((None, None, tq, LANES), cur)],
            out_shape=[jax.ShapeDtypeStruct((b, dil, rows, d), BF16),
                       jax.ShapeDtypeStruct((b, dil, rows, LANES), F32)],
            compiler_params=_params(3),
            name=f"dilated_prompt_d{dil}",
        )(q, k, k, v, v)
        outs.append(o_g)
        lses.append(lse_g)
    return outs, lses


def _dilated_sample_body(q_ref, kn_ref, vn_ref, k_ref, v_ref, o_ref, qb_ref, knb_ref, vnb_ref):
    j = pl.program_id(1)
    hg, rows = k_ref.shape[1], k_ref.shape[3]

    @pl.when(j == 0)
    def _():
        qb_ref[...] = _col_rep(q_ref[0])
        knb_ref[...] = _col_rep(kn_ref[0].astype(BF16))
        vnb_ref[...] = _col_rep(vn_ref[0].astype(BF16))

    idx = lax.broadcasted_iota(jnp.int32, (1, rows), 1)
    mult = jnp.zeros((1, rows), F32)
    for dil in DILATIONS:
        in_branch = (idx >= rows - WINDOW_KEYS * dil) & (lax.rem(rows - idx, dil) == 0)
        mult = mult + jnp.where(in_branch, 1.0, 0.0)
    valid = mult > 0.0
    n_self = float(len(DILATIONS))
    cols = []
    for h in range(hg):
        qh = qb_ref[j * hg + h]
        s = jnp.sum(k_ref[0, h] * jnp.tile(qh, (1, rows // LANES)), axis=0, keepdims=True)
        s_self = jnp.sum(knb_ref[j * hg + h] * qh, axis=0, keepdims=True)[:, 0:1]
        m = jnp.maximum(jnp.max(jnp.where(valid, s, NEG_INF), axis=1, keepdims=True), s_self)
        pr = jnp.where(valid, mult * jnp.exp(s - m), 0.0)
        p_self = n_self * jnp.exp(s_self - m)
        l = jnp.sum(pr, axis=1, keepdims=True) + p_self
        col = jnp.sum(v_ref[0, h] * pr, axis=1, keepdims=True) + p_self * vnb_ref[j * hg + h][:, 0:1]
        cols.append(col / l)
    o_ref[0] = _cols_to_rows(cols).astype(BF16)


def _dilated_sample(q, k_new, v_new, cache_kt, cache_vt, hg=8):
    n, d = q.shape
    _, nh, hd, rows = cache_kt.shape
    seq3 = lambda bi, j: (bi, 0, 0)
    slab = pl.BlockSpec((1, hg, hd, rows), lambda bi, j: (bi, j, 0, 0))
    row_spec = pl.BlockSpec((1, 1, d), seq3)
    rep = pltpu.VMEM((nh, hd, LANES), F32)
    out = pl.pallas_call(
        _dilated_sample_body,
        grid=(n, nh // hg),
        in_specs=[row_spec, row_spec, row_spec, slab, slab],
        out_specs=pl.BlockSpec((1, hg, hd), lambda bi, j: (bi, j, 0)),
        out_shape=jax.ShapeDtypeStruct((n, nh, hd), BF16),
        scratch_shapes=[rep, rep, rep],
        compiler_params=_params(2),
        name="dilated_sample",
    )(q.reshape(n, 1, d), k_new.reshape(n, 1, d), v_new.reshape(n, 1, d), cache_kt, cache_vt)
    return out.reshape(n, d)


def kernel(x_prompt, x_sample, cache_k_a, cache_v_a, cache_logf_a, page_table, cache_k_b, cache_v_b, state_conv, c_prompt, c_sample, w_mod, b_mod, g_norm, w_in_a, b_f_a, w_out_a, g_kv, w_kv_b, w_q_b, w_out_b, w_up, w_conv, b_conv, w_down):
    bp, sp, d = x_prompt.shape
    ns = x_sample.shape[0]
    depth = w_mod.shape[0]
    assert depth == 2 and x_sample.shape[1] == 1 and d == N_HEADS * HEAD_DIM and cache_k_a.shape[0] == 1
    n_pool, page = cache_k_a.shape[1], cache_k_a.shape[2]
    past_len = page_table.shape[1] * page
    wb = cache_k_b.shape[1]
    assert wb == W_MAX and sp >= W_MAX

    pad = (-(bp + ns)) % SUBLANES
    c_all = jnp.concatenate([c_prompt, c_sample, jnp.zeros((pad, d), F32)], axis=0)
    mod = _modulation(c_all, w_mod, b_mod).reshape(depth, bp + ns + pad, 6, d)
    mod_p = [[mod[l, :bp, j].reshape(bp, 1, d) for j in range(6)] for l in range(depth)]
    mod_s = [[mod[l, bp:bp + ns, j].reshape(1, ns, d) for j in range(6)] for l in range(depth)]

    w_qkv_a = w_in_a[0, :, :3 * d].astype(BF16)
    w_f_a = w_in_a[0, :, 3 * d:].astype(BF16)
    w_out_a16 = w_out_a[0].astype(BF16)
    w_kv16 = w_kv_b.astype(BF16)
    w_q16 = w_q_b[0].astype(BF16)
    w_out_b16 = w_out_b[0].astype(BF16)
    w_up16 = w_up.astype(BF16)
    w_down16 = w_down.astype(BF16)

    cos_p, sin_p = _rope_tables(jnp.arange(sp))
    cos_s, sin_s = _rope_tables(jnp.full((ns,), past_len, jnp.int32))

    cache_kt_a = jnp.transpose(cache_k_a.reshape(n_pool, page, N_HEADS, HEAD_DIM), (0, 2, 3, 1))
    cache_vt_a = jnp.transpose(cache_v_a.reshape(n_pool, page, N_HEADS, HEAD_DIM), (0, 2, 3, 1))
    cache_lft_a = jnp.transpose(cache_logf_a.reshape(n_pool, page, N_HEADS), (0, 2, 1))
    cache_kt_b = jnp.transpose(cache_k_b, (0, 2, 3, 1))
    cache_vt_b = jnp.transpose(cache_v_b, (0, 2, 3, 1))

    sh, sc, gt, shf, scf, gtf = mod_p[0]
    q_pm, k_pm, v_pm, ka_p, va_p, lfa_p, drow = _fox_proj_prompt(
        x_prompt, g_norm[0, 0], sc, sh, w_qkv_a, w_f_a, b_f_a[0])
    o_p = _fox_attn_prompt(q_pm, k_pm, v_pm, drow)
    h_p, cv_p0 = _post(x_prompt, o_p, w_out_a16, g_norm[0], gt, scf, shf, gtf,
                       w_up16[0], w_conv[0], b_conv[0], w_down16[0])

    sh_s, sc_s, gt_s, shf_s, scf_s, gtf_s = mod_s[0]
    xs = x_sample.reshape(ns, d)
    q_s, ka_s, va_s, lfa_s = _fox_proj_sample(
        xs, g_norm[0, 0], sc_s.reshape(ns, d), sh_s.reshape(ns, d), w_qkv_a, w_f_a, b_f_a[0])
    o_s = _fox_attn_sample(q_s, ka_s, va_s, lfa_s, cache_kt_a, cache_vt_a, cache_lft_a, page_table)
    prev0 = (state_conv[0][:, 0].reshape(1, ns, -1), state_conv[0][:, 1].reshape(1, ns, -1))
    h_s, a_s0 = _post(xs.reshape(1, ns, d), o_s.reshape(1, ns, d), w_out_a16, g_norm[0], gt_s, scf_s, shf_s, gtf_s,
                      w_up16[0], w_conv[0], b_conv[0], w_down16[0], prev=prev0)

    sh, sc, gt, shf, scf, gtf = mod_p[1]
    q_cls, k_cls, v_cls, kb_p, vb_p = _kvq(h_p, g_kv, g_norm[1, 0], sc, sh, w_kv16, w_q16, cos_p, sin_p,
                                           keep_rows=min(W_MAX, sp), dils=DILATIONS)
    sh_s, sc_s, gt_s, shf_s, scf_s, gtf_s = mod_s[1]
    (qb_s,), _, _, kb_s, vb_s = _kvq(h_s, g_kv, g_norm[1, 0], sc_s, sh_s, w_kv16, w_q16, cos_s, sin_s, keep_rows=ns)

    o1_p, lse_p = _dilated_prompt(q_cls, k_cls, v_cls)
    y_p, cv_p1 = _post(h_p, o1_p, w_out_b16, g_norm[1], gt, scf, shf, gtf,
                       w_up16[1], w_conv[1], b_conv[1], w_down16[1], lse=lse_p)
    o1_s = _dilated_sample(qb_s.reshape(ns, d), kb_s.reshape(ns, d), vb_s.reshape(ns, d), cache_kt_b, cache_vt_b)
    prev1 = (state_conv[1][:, 0].reshape(1, ns, -1), state_conv[1][:, 1].reshape(1, ns, -1))
    y_s, a_s1 = _post(h_s, o1_s.reshape(1, ns, d), w_out_b16, g_norm[1], gt_s, scf_s, shf_s, gtf_s,
                      w_up16[1], w_conv[1], b_conv[1], w_down16[1], prev=prev1)

    heads = (N_HEADS, HEAD_DIM)
    new_conv_sample = jnp.stack([
        jnp.stack([state_conv[0][:, 1], a_s0.reshape(ns, -1)], axis=1),
        jnp.stack([state_conv[1][:, 1], a_s1.reshape(ns, -1)], axis=1),
    ])
    return (
        y_p,
        y_s.reshape(ns, 1, d),
        ka_p.reshape(1, bp, sp, *heads),
        va_p.reshape(1, bp, sp, *heads),
        lfa_p.reshape(1, bp, sp, N_HEADS),
        ka_s.reshape(1, ns, 1, *heads),
        va_s.reshape(1, ns, 1, *heads),
        lfa_s.reshape(1, ns, 1, N_HEADS),
        kb_p.reshape(bp, -1, *heads),
        vb_p.reshape(bp, -1, *heads),
        kb_s.reshape(ns, 1, *heads),
        vb_s.reshape(ns, 1, *heads),
        jnp.stack([cv_p0, cv_p1]),
        new_conv_sample,
    )
```

```python
import functools

import jax
import jax.numpy as jnp
from jax import lax
from jax.experimental import pallas as pl
from jax.experimental.pallas import tpu as pltpu

F32 = jnp.float32
BF16 = jnp.bfloat16

N_HEADS = 16
HEAD_DIM = 64
DILATIONS = (1, 4, 16)
WINDOW_KEYS = 128
W_MAX = 2048
ROPE_THETA = 10000.0
NORM_EPS = 1e-6
NEG_INF = -1e30
SCALE = HEAD_DIM ** -0.5
LOG2_E = 1.4426950408889634

LANES = 128
SUBLANES = 8
PAIR = 2 * HEAD_DIM
VMEM_LIMIT_BYTES = 56 * 1024 * 1024
DOWN_PIECES = 2

_NT = (((1,), (1,)), ((), ()))
_TN = (((0,), (0,)), ((), ()))


def _params(n_grid_axes):
    return pltpu.CompilerParams(
        dimension_semantics=("arbitrary",) * n_grid_axes,
        vmem_limit_bytes=VMEM_LIMIT_BYTES,
    )


def _resident(shape):
    nd = len(shape)
    return pl.BlockSpec(shape, lambda *_: (0,) * nd, pipeline_mode=pl.Buffered(1))


def _rms_hat(x):
    return x * lax.rsqrt(jnp.mean(x * x, axis=-1, keepdims=True) + NORM_EPS)


def _log_sigmoid(x):
    return jnp.minimum(x, 0.0) - jnp.log1p(jnp.exp(-jnp.abs(x)))


def _gelu_tanh(x):
    return x * (0.5 * (1.0 + jnp.tanh(0.7978845608028654 * (x + 0.044715 * (x * x * x)))))


def _split3(x):
    hi = x.astype(BF16)
    r = x - hi.astype(F32)
    mid = r.astype(BF16)
    lo = (r - mid.astype(F32)).astype(BF16)
    return hi, mid, lo


def _dot(a, b):
    return jnp.dot(a, b, preferred_element_type=F32)


def _rope(x, cos, sin):
    d = x.shape[-1]
    lane = lax.broadcasted_iota(jnp.int32, x.shape, 1)
    first = (lane % HEAD_DIM) < (HEAD_DIM // 2)
    partner = jnp.where(first, pltpu.roll(x, d - HEAD_DIM // 2, 1), pltpu.roll(x, HEAD_DIM // 2, 1))
    reps = d // LANES
    return x * jnp.tile(cos, (1, reps)) + partner * jnp.tile(sin, (1, reps))


def _rope_tables(pos):
    half = HEAD_DIM // 2
    inv = jnp.power(ROPE_THETA, -jnp.arange(half, dtype=F32) * (2.0 / HEAD_DIM))
    ang = pos.astype(F32)[:, None] * inv[None, :]
    cos, sin = jnp.cos(ang), jnp.sin(ang)
    cos_h = jnp.concatenate([cos, cos], axis=-1)
    sin_h = jnp.concatenate([-sin, sin], axis=-1)
    return jnp.tile(cos_h, (1, LANES // HEAD_DIM)), jnp.tile(sin_h, (1, LANES // HEAD_DIM))


def _col_rep(row):
    d = row.shape[1]
    first_row = lax.broadcasted_iota(jnp.int32, (SUBLANES, LANES), 0) == 0
    e0 = jnp.where(first_row, 1.0, 0.0).astype(BF16)
    col = lax.dot_general(jnp.broadcast_to(row, (SUBLANES, d)), e0, _TN, preferred_element_type=F32)
    return col.reshape(d // HEAD_DIM, HEAD_DIM, LANES)


def _cols_to_rows(cols):
    lane = lax.broadcasted_iota(jnp.int32, (HEAD_DIM, LANES), 1)
    packed = jnp.zeros((HEAD_DIM, LANES), F32)
    for j, col in enumerate(cols):
        packed = jnp.where(lane == j, col, packed)
    eye = (lax.broadcasted_iota(jnp.int32, (HEAD_DIM, HEAD_DIM), 0)
           == lax.broadcasted_iota(jnp.int32, (HEAD_DIM, HEAD_DIM), 1))
    eye = jnp.where(eye, 1.0, 0.0).astype(BF16)
    rows = lax.dot_general(packed.astype(BF16), eye, _TN, preferred_element_type=F32)
    return rows[0:len(cols)]


def _mod_body(c_ref, w_ref, b_ref, o_ref):
    c = c_ref[...]
    s = (c * jax.nn.sigmoid(c)).astype(BF16)
    o_ref[0] = _dot(s, w_ref[0].astype(BF16)) + b_ref[0]


def _modulation(c_all, w_mod, b_mod):
    depth, d, n = w_mod.shape
    r = c_all.shape[0]
    tn = 1536
    return pl.pallas_call(
        _mod_body,
        grid=(depth, n // tn),
        in_specs=[
            pl.BlockSpec((r, d), lambda l, j: (0, 0)),
            pl.BlockSpec((1, d, tn), lambda l, j: (l, 0, j)),
            pl.BlockSpec((1, 1, tn), lambda l, j: (l, 0, j)),
        ],
        out_specs=pl.BlockSpec((1, r, tn), lambda l, j: (l, 0, j)),
        out_shape=jax.ShapeDtypeStruct((depth, r, n), F32),
        compiler_params=_params(2),
        name="modulation",
    )(c_all, w_mod, b_mod.reshape(depth, 1, n))


def _fox_proj_prompt_body(x_ref, g_ref, sc_ref, sh_ref, w_ref, wf_ref, wft_ref, bf_ref, bft_ref, tri_ref,
                          q_ref, k_ref, v_ref, kf_ref, vf_ref, lf_ref, d_ref, carry_ref):
    i = pl.program_id(1)
    x = x_ref[0]
    tm, dm = x.shape
    u = _rms_hat(x) * g_ref[...] * (1.0 + sc_ref[0]) + sh_ref[0]
    ub = u.astype(BF16)
    qkv = _dot(ub, w_ref[...])
    k = qkv[:, dm:2 * dm]
    v = qkv[:, 2 * dm:]
    kf_ref[0] = k
    vf_ref[0] = v
    qb = (qkv[:, :dm] * (SCALE * LOG2_E)).astype(BF16)
    kb = k.astype(BF16)
    vb = v.astype(BF16)
    for p in range(dm // PAIR):
        sl = slice(p * PAIR, (p + 1) * PAIR)
        q_ref[0, p] = qb[:, sl]
        k_ref[0, p] = kb[:, sl]
        v_ref[0, p] = vb[:, sl]
    lf_ref[0] = _log_sigmoid(_dot(ub, wf_ref[...]) + bf_ref[...])
    lft = _log_sigmoid(lax.dot_general(wft_ref[...], ub, _NT, preferred_element_type=F32) + bft_ref[...])
    tri = tri_ref[...]
    hi, mid, lo = _split3(lft)
    dloc = _dot(hi, tri) + _dot(mid, tri) + _dot(lo, tri)

    @pl.when(i == 0)
    def _():
        carry_ref[...] = jnp.zeros_like(carry_ref)

    dfull = dloc + carry_ref[...]
    carry_ref[...] = dfull[:, tm - 1:tm]
    for h in range(N_HEADS):
        d_ref[0, h] = dfull[h:h + 1, :]


def _fox_proj_prompt(x, g, sc, sh, w_qkv, w_f, b_f, tm=512):
    b, s, d = x.shape
    npair = d // PAIR
    nh = w_f.shape[1]
    tri = (jnp.arange(tm)[:, None] <= jnp.arange(tm)[None, :]).astype(BF16)
    pm_shape = jax.ShapeDtypeStruct((b, npair, s, PAIR), BF16)
    pm_spec = pl.BlockSpec((1, npair, tm, PAIR), lambda bi, i: (bi, 0, i, 0))
    row_spec = pl.BlockSpec((1, tm, d), lambda bi, i: (bi, i, 0))
    mod_spec = pl.BlockSpec((1, 1, d), lambda bi, i: (bi, 0, 0))
    return pl.pallas_call(
        _fox_proj_prompt_body,
        grid=(b, s // tm),
        in_specs=[
            row_spec, _resident((1, d)), mod_spec, mod_spec,
            _resident((d, 3 * d)), _resident((d, nh)), _resident((nh, d)),
            _resident((1, nh)), _resident((nh, 1)), _resident((tm, tm)),
        ],
        out_specs=[
            pm_spec, pm_spec, pm_spec, row_spec, row_spec,
            pl.BlockSpec((1, tm, nh), lambda bi, i: (bi, i, 0)),
            pl.BlockSpec((1, nh, 1, tm), lambda bi, i: (bi, 0, 0, i)),
        ],
        out_shape=[
            pm_shape, pm_shape, pm_shape,
            jax.ShapeDtypeStruct((b, s, d), F32), jax.ShapeDtypeStruct((b, s, d), F32),
            jax.ShapeDtypeStruct((b, s, nh), F32), jax.ShapeDtypeStruct((b, nh, 1, s), F32),
        ],
        scratch_shapes=[pltpu.VMEM((nh, 1), F32)],
        compiler_params=_params(2),
        name="fox_proj_prompt",
    )(x, g.reshape(1, d), sc, sh, w_qkv, w_f, w_f.T, b_f.reshape(1, nh), b_f.reshape(nh, 1), tri)


def _fox_proj_sample_body(x_ref, g_ref, sc_ref, sh_ref, w_ref, wf_ref, bf_ref, q_ref, k_ref, v_ref, lf_ref):
    x = x_ref[...]
    dm = x.shape[1]
    u = _rms_hat(x) * g_ref[...] * (1.0 + sc_ref[...]) + sh_ref[...]
    ub = u.astype(BF16)
    qkv = _dot(ub, w_ref[...])
    q_ref[...] = (qkv[:, :dm] * SCALE).astype(BF16)
    k_ref[...] = qkv[:, dm:2 * dm]
    v_ref[...] = qkv[:, 2 * dm:]
    lf_ref[...] = _log_sigmoid(_dot(ub, wf_ref[...]) + bf_ref[...])


def _fox_proj_sample(x, g, sc, sh, w_qkv, w_f, b_f):
    n, d = x.shape
    nh = w_f.shape[1]
    return pl.pallas_call(
        _fox_proj_sample_body,
        out_shape=[
            jax.ShapeDtypeStruct((n, d), BF16), jax.ShapeDtypeStruct((n, d), F32),
            jax.ShapeDtypeStruct((n, d), F32), jax.ShapeDtypeStruct((n, nh), F32),
        ],
        compiler_params=pltpu.CompilerParams(vmem_limit_bytes=VMEM_LIMIT_BYTES),
        name="fox_proj_sample",
    )(x, g.reshape(1, d), sc, sh, w_qkv, w_f, b_f.reshape(1, nh))


def _fox_attn_body(qi_ref, kj_ref, q_ref, k_ref, v_ref, dk_ref, dq_ref, o_ref, m_ref, acc_ref):
    t = pl.program_id(1)
    qi = qi_ref[t]
    kj = kj_ref[t]
    npair, tq = q_ref.shape[1], q_ref.shape[2]
    tk = k_ref.shape[2]
    lo_half = lax.broadcasted_iota(jnp.int32, (tq, PAIR), 1) < HEAD_DIM
    lo_half_k = lax.broadcasted_iota(jnp.int32, (tk, PAIR), 1) < HEAD_DIM

    @pl.when(kj == 0)
    def _():
        m_ref[...] = jnp.full(m_ref.shape, NEG_INF, F32)
        acc_ref[...] = jnp.zeros_like(acc_ref)

    def run(masked):
        if masked:
            keep = (lax.broadcasted_iota(jnp.int32, (tq, tk), 1)
                    <= lax.broadcasted_iota(jnp.int32, (tq, tk), 0))

        def pair_body(p, carry):
            qp = q_ref[0, p]
            kp = k_ref[0, p]
            vp = v_ref[0, p]
            for hh in range(2):
                h = 2 * p + hh
                sel = lo_half if hh == 0 else jnp.logical_not(lo_half)
                sel_k = lo_half_k if hh == 0 else jnp.logical_not(lo_half_k)
                qh = jnp.where(sel, qp, jnp.zeros_like(qp))
                s = lax.dot_general(qh, kp, _NT, preferred_element_type=F32)
                s = s + (dq_ref[0, h][:, 0:1] - dk_ref[0, h]) * LOG2_E
                if masked:
                    s = jnp.where(keep, s, NEG_INF)
                m_prev = m_ref[h]
                m_new = jnp.maximum(m_prev, jnp.max(s, axis=1, keepdims=True))
                alpha = jnp.exp2(m_prev - m_new)
                pr = jnp.exp2(s - jnp.tile(m_new, (1, tk // LANES)))
                m_ref[h] = m_new
                vh = jnp.where(sel_k, vp, jnp.ones_like(vp))
                acc_ref[h] = acc_ref[h] * alpha + _dot(pr.astype(BF16), vh)
            return carry

        lax.fori_loop(0, npair, pair_body, 0, unroll=4)

    @pl.when(kj < qi)
    def _():
        run(False)

    @pl.when(kj == qi)
    def _():
        run(True)
        for p in range(npair):
            a0 = acc_ref[2 * p]
            a1 = acc_ref[2 * p + 1]
            o0 = a0 / pltpu.roll(a0, HEAD_DIM, 1)
            o1 = a1 / pltpu.roll(a1, HEAD_DIM, 1)
            o_ref[0, :, p * PAIR:(p + 1) * PAIR] = jnp.where(lo_half, o0, o1).astype(BF16)


def _fox_attn_prompt(q_pm, k_pm, v_pm, drow, t=512):
    b, npair, s, _ = q_pm.shape
    nh = drow.shape[1]
    nt = s // t
    pairs = [(i, j) for i in range(nt) for j in range(i + 1)]
    qi = jnp.asarray([p[0] for p in pairs], jnp.int32)
    kj = jnp.asarray([p[1] for p in pairs], jnp.int32)
    grid_spec = pltpu.PrefetchScalarGridSpec(
        num_scalar_prefetch=2,
        grid=(b, len(pairs)),
        in_specs=[
            pl.BlockSpec((1, npair, t, PAIR), lambda bi, n, qi, kj: (bi, 0, qi[n], 0)),
            pl.BlockSpec((1, npair, t, PAIR), lambda bi, n, qi, kj: (bi, 0, kj[n], 0)),
            pl.BlockSpec((1, npair, t, PAIR), lambda bi, n, qi, kj: (bi, 0, kj[n], 0)),
            pl.BlockSpec((1, nh, 1, t), lambda bi, n, qi, kj: (bi, 0, 0, kj[n])),
            pl.BlockSpec((1, nh, 1, LANES), lambda bi, n, qi, kj: (bi, 0, 0, qi[n] * (t // LANES))),
        ],
        out_specs=pl.BlockSpec((1, t, npair * PAIR), lambda bi, n, qi, kj: (bi, qi[n], 0)),
        scratch_shapes=[pltpu.VMEM((nh, t, LANES), F32), pltpu.VMEM((nh, t, PAIR), F32)],
    )
    return pl.pallas_call(
        _fox_attn_body,
        grid_spec=grid_spec,
        out_shape=jax.ShapeDtypeStruct((b, s, npair * PAIR), BF16),
        compiler_params=_params(2),
        name="fox_attn_prompt",
    )(qi, kj, q_pm, k_pm, v_pm, drow, drow)


def _fox_decode_body(pt_ref, q_ref, kn_ref, vn_ref, lfn_ref, *refs, pages):
    k_refs = refs[:pages]
    v_refs = refs[pages:2 * pages]
    lf_refs = refs[2 * pages:3 * pages]
    o_ref, qb_ref, m_ref, l_ref, acc_ref, carry_ref, s_scr = refs[3 * pages:]
    g = pl.program_id(1)
    nh, page = lf_refs[0].shape[1], lf_refs[0].shape[2]
    lane = lax.broadcasted_iota(jnp.int32, (nh, page), 1)

    @pl.when(g == 0)
    def _():
        qb_ref[...] = _col_rep(q_ref[0])
        m_ref[...] = jnp.full(m_ref.shape, NEG_INF, F32)
        l_ref[...] = jnp.zeros_like(l_ref)
        acc_ref[...] = jnp.zeros_like(acc_ref)
        carry_ref[...] = jnp.zeros_like(carry_ref)

    for u in reversed(range(pages)):
        lft = lf_refs[u][0]
        suf = lft
        shift = 1
        while shift < page:
            suf = suf + jnp.where(lane + shift < page, pltpu.roll(suf, page - shift, 1), 0.0)
            shift *= 2
        carry = carry_ref[...]
        s_scr[:, u * page:(u + 1) * page] = (suf - lft) + carry
        carry_ref[...] = carry + suf[:, 0:1]
    for h in range(nh):
        qh = qb_ref[h]
        for u in range(pages):
            sl = slice(u * page, (u + 1) * page)
            s_scr[h:h + 1, sl] = s_scr[h:h + 1, sl] + jnp.sum(k_refs[u][0, h] * qh, axis=0, keepdims=True)
    s = s_scr[...]
    m_prev = m_ref[...]
    m_new = jnp.maximum(m_prev, jnp.max(s, axis=1, keepdims=True))
    alpha = jnp.exp(m_prev - m_new)
    pr = jnp.exp(s - m_new)
    l_ref[...] = alpha * l_ref[...] + jnp.sum(pr, axis=1, keepdims=True)
    m_ref[...] = m_new
    for h in range(nh):
        acc = acc_ref[h] * alpha[h:h + 1, :]
        for u in range(pages):
            acc = acc + pr[h:h + 1, u * page:(u + 1) * page] * v_refs[u][0, h]
        acc_ref[h] = acc

    @pl.when(g == pl.num_programs(1) - 1)
    def _():
        knb = _col_rep(kn_ref[0].astype(BF16))
        vnb = _col_rep(vn_ref[0].astype(BF16))
        for h in range(nh):
            s_scr[h:h + 1, 0:page] = jnp.sum(knb[h] * qb_ref[h], axis=0, keepdims=True)
        s_new = jnp.where(lane == 0, s_scr[:, 0:page] - lfn_ref[0], NEG_INF)
        m_old = m_ref[...]
        m_fin = jnp.maximum(m_old, jnp.max(s_new, axis=1, keepdims=True))
        a_fin = jnp.exp(m_old - m_fin)
        p_new = jnp.exp(s_new - m_fin)
        l_fin = a_fin * l_ref[...] + jnp.sum(p_new, axis=1, keepdims=True)
        cols = []
        for h in range(nh):
            acc = acc_ref[h] * a_fin[h:h + 1, :] + p_new[h:h + 1, :] * vnb[h]
            cols.append(jnp.sum(acc, axis=1, keepdims=True) / l_fin[h:h + 1, :])
        o_ref[0] = _cols_to_rows(cols).astype(BF16)


def _fox_attn_sample(q, k_new, v_new, lf_new, cache_kt, cache_vt, cache_lft, page_table, pages=16):
    n, d = q.shape
    _, nh, hd, page = cache_kt.shape
    n_groups = page_table.shape[1] // pages

    def page_map(nd, u):
        return lambda bi, g, pt: (pt[bi, (n_groups - 1 - g) * pages + u],) + (0,) * (nd - 1)

    seq3 = lambda bi, g, pt: (bi, 0, 0)
    kv_specs = [pl.BlockSpec((1, nh, hd, page), page_map(4, u)) for u in range(pages)]
    lf_specs = [pl.BlockSpec((1, nh, page), page_map(3, u)) for u in range(pages)]
    grid_spec = pltpu.PrefetchScalarGridSpec(
        num_scalar_prefetch=1,
        grid=(n, n_groups),
        in_specs=[
            pl.BlockSpec((1, 1, d), seq3), pl.BlockSpec((1, 1, d), seq3), pl.BlockSpec((1, 1, d), seq3),
            pl.BlockSpec((1, nh, 1), seq3),
        ] + kv_specs + kv_specs + lf_specs,
        out_specs=pl.BlockSpec((1, nh, hd), seq3),
        scratch_shapes=[
            pltpu.VMEM((nh, hd, LANES), F32), pltpu.VMEM((nh, 1), F32), pltpu.VMEM((nh, 1), F32),
            pltpu.VMEM((nh, hd, LANES), F32), pltpu.VMEM((nh, 1), F32), pltpu.VMEM((nh, pages * page), F32),
        ],
    )
    out = pl.pallas_call(
        functools.partial(_fox_decode_body, pages=pages),
        grid_spec=grid_spec,
        out_shape=jax.ShapeDtypeStruct((n, nh, hd), BF16),
        compiler_params=_params(2),
        name="fox_attn_sample",
    )(page_table, q.reshape(n, 1, d), k_new.reshape(n, 1, d), v_new.reshape(n, 1, d), lf_new.reshape(n, nh, 1),
      *([cache_kt] * pages), *([cache_vt] * pages), *([cache_lft] * pages))
    return out.reshape(n, d)


def _post_body(*refs, prompt, chunk, n_br):
    x_ref = refs[0]
    o_refs = refs[1:1 + n_br]
    refs = refs[1 + n_br:]
    if n_br > 1:
        lse_refs = refs[:n_br]
        e_ref = refs[n_br]
        refs = refs[n_br + 1:]
    wo_ref, g_ref, gt_ref, scf_ref, shf_ref, gtf_ref, wup_ref, wc_ref, bc_ref, wdn_ref = refs[:10]
    refs = refs[10:]
    if prompt:
        h_ref, cv_ref, a_scr, act_scr = refs[:4]
    else:
        p0_ref, p1_ref, h_ref, cv_ref, act_scr = refs[:5]
    x = x_ref[0]
    tm, dm = x.shape
    dff = wdn_ref.shape[0]
    g = g_ref[...]
    if n_br == 1:
        o = o_refs[0][0]
    else:
        nat_scr, lse_scr = refs[-2:]
        nslab = dm // LANES
        outs, lses = [], []
        ci = 0
        for dil, o_ref, lse_ref in zip(DILATIONS, o_refs, lse_refs):
            if dil == 1:
                outs.append(o_ref[0, 0].astype(F32))
                lses.append(lse_ref[0, 0])
                continue
            for r in range(dil):
                lse_scr[ci, pl.ds(r, tm // dil, stride=dil), :] = lse_ref[0, r]
                for j in range(nslab):
                    rows = o_ref[0, r, :, j * LANES:(j + 1) * LANES].astype(F32)
                    nat_scr[ci, j, pl.ds(r, tm // dil, stride=dil), :] = rows
            outs.append(jnp.concatenate([nat_scr[ci, j] for j in range(nslab)], axis=1))
            lses.append(lse_scr[ci])
            ci += 1
        top = functools.reduce(jnp.maximum, lses)
        es = [jnp.exp2(l - top) for l in lses]
        den = functools.reduce(lambda a, b: a + b, es)
        o = None
        for e, o_g in zip(es, outs):
            term = _dot((e / den).astype(BF16), e_ref[...]) * o_g
            o = term if o is None else o + term
        o = o.astype(BF16)
    att = _dot(o, wo_ref[...])
    h1 = x + gt_ref[0] * (_rms_hat(att) * g[1:2])
    u = (_rms_hat(h1) * g[2:3] * (1.0 + scf_ref[0]) + shf_ref[0]).astype(BF16)
    if prompt:
        i = pl.program_id(1)
        halo = a_scr.shape[0] - tm

        @pl.when(i == 0)
        def _():
            a_scr[0:halo, :] = jnp.zeros((halo, a_scr.shape[1]), F32)

    n_chunks = dff // chunk
    flush_at = [((k + 1) * n_chunks) // DOWN_PIECES - 1 for k in range(DOWN_PIECES)]
    ffn = None
    start = 0
    for c in range(n_chunks):
        ys = []
        for base in (0, dff):
            sl = slice(base + c * chunk, base + (c + 1) * chunk)
            a = _dot(u, wup_ref[:, sl])
            if prompt:
                a_scr[halo:halo + tm, sl] = a
                am2 = a_scr[halo - 2:halo - 2 + tm, sl]
                am1 = a_scr[halo - 1:halo - 1 + tm, sl]
            else:
                cv_ref[0, :, sl] = a
                am2 = p0_ref[0, :, sl]
                am1 = p1_ref[0, :, sl]
            ys.append(bc_ref[:, sl] + wc_ref[0:1, sl] * am2 + wc_ref[1:2, sl] * am1 + wc_ref[2:3, sl] * a)
        act_scr[:, c * chunk:(c + 1) * chunk] = (_gelu_tanh(ys[0]) * ys[1]).astype(BF16)
        if c in flush_at:
            stop = (c + 1) * chunk
            part = _dot(act_scr[:, start:stop], wdn_ref[start:stop, :])
            ffn = part if ffn is None else ffn + part
            start = stop
    if prompt:
        @pl.when(i == pl.num_programs(1) - 1)
        def _():
            cv_ref[0] = a_scr[halo + tm - 2:halo + tm, :]

        a_scr[0:halo, :] = a_scr[tm:tm + halo, :]
    h_ref[0] = h1 + gtf_ref[0] * (_rms_hat(ffn) * g[3:4])


def _post(x, o, w_out, g4, gt, scf, shf, gtf, w_up, w_conv, b_conv, w_down, lse=None, prev=None,
          tm=256, chunk=256):
    b, s, d = x.shape
    dff = w_down.shape[0]
    prompt = prev is None
    o_list = list(o) if isinstance(o, (list, tuple)) else [o]
    n_br = len(o_list)
    tm = min(tm, s)
    halo = SUBLANES
    row = lambda bi, i: (bi, i, 0)

    def mod_spec(a):
        per_row = a.shape[1] == s and s > 1
        return pl.BlockSpec((1, tm if per_row else 1, d), row if per_row else (lambda bi, i: (bi, 0, 0)))

    in_specs = [pl.BlockSpec((1, tm, d), row)]
    args = [x] + o_list
    scratch_mix = []
    if n_br == 1:
        in_specs += [pl.BlockSpec((1, tm, d), row)]
    else:
        head_of_lane = jnp.arange(d)[None, :] // HEAD_DIM
        expand = (jnp.arange(LANES)[:, None] == head_of_lane).astype(BF16)
        cls = lambda bi, i: (bi, 0, i, 0)
        in_specs += [pl.BlockSpec((1, dil, tm // dil, d), cls) for dil in DILATIONS]
        in_specs += [pl.BlockSpec((1, dil, tm // dil, LANES), cls) for dil in DILATIONS]
        in_specs += [_resident((LANES, d))]
        args += list(lse) + [expand]
        n_cls = sum(1 for dil in DILATIONS if dil > 1)
        scratch_mix = [pltpu.VMEM((n_cls, d // LANES, tm, LANES), F32), pltpu.VMEM((n_cls, tm, LANES), F32)]
    in_specs += [
        _resident((d, d)), _resident((4, d)), mod_spec(gt), mod_spec(scf), mod_spec(shf), mod_spec(gtf),
        _resident((d, 2 * dff)), _resident((3, 2 * dff)), _resident((1, 2 * dff)), _resident((dff, d)),
    ]
    args += [w_out, g4, gt, scf, shf, gtf, w_up, w_conv, b_conv.reshape(1, 2 * dff), w_down]
    scratch = [pltpu.VMEM((tm, dff), BF16)]
    if prompt:
        cv_shape = jax.ShapeDtypeStruct((b, 2, 2 * dff), F32)
        cv_spec = pl.BlockSpec((1, 2, 2 * dff), lambda bi, i: (bi, 0, 0))
        scratch = [pltpu.VMEM((tm + halo, 2 * dff), F32)] + scratch
    else:
        cv_shape = jax.ShapeDtypeStruct((b, s, 2 * dff), F32)
        cv_spec = pl.BlockSpec((1, tm, 2 * dff), row)
        in_specs += [pl.BlockSpec((1, tm, 2 * dff), row)] * 2
        args += [prev[0], prev[1]]
    return pl.pallas_call(
        functools.partial(_post_body, prompt=prompt, chunk=chunk, n_br=n_br),
        grid=(b, s // tm),
        in_specs=in_specs,
        out_specs=[pl.BlockSpec((1, tm, d), row), cv_spec],
        out_shape=[jax.ShapeDtypeStruct((b, s, d), F32), cv_shape],
        scratch_shapes=scratch + scratch_mix,
        compiler_params=_params(2),
        name="post_prompt" if prompt else "post_sample",
    )(*args)


def _kvq_body(x_ref, gkv_ref, gq_ref, sc_ref, sh_ref, wkv_ref, wq_ref, cos_ref, sin_ref, *refs, n_skip, dils):
    nd = len(dils)
    q_refs, k_refs, v_refs = refs[:nd], refs[nd:2 * nd], refs[2 * nd:3 * nd]
    kf_ref, vf_ref = refs[3 * nd:3 * nd + 2]
    i = pl.program_id(1)
    x = x_ref[0]
    tm, dm = x.shape
    xh = _rms_hat(x)
    cos = cos_ref[...]
    sin = sin_ref[...]
    kv = _dot((xh * gkv_ref[...]).astype(BF16), wkv_ref[...])
    k = _rope(kv[:, :dm], cos, sin)
    v = kv[:, dm:]
    uq = (xh * gq_ref[...] * (1.0 + sc_ref[0]) + sh_ref[0]).astype(BF16)
    q = _rope(_dot(uq, wq_ref[...]), cos, sin) * (SCALE * LOG2_E)

    @pl.when(i >= n_skip)
    def _():
        kf_ref[0] = k
        vf_ref[0] = v

    nslab = dm // LANES
    for val, outs in ((q, q_refs), (k, k_refs), (v, v_refs)):
        outs[0][0, 0] = val.astype(BF16)
        if nd > 1:
            stages = refs[3 * nd + 2:]
            for j in range(nslab):
                stages[0][0, j] = val[:, j * LANES:(j + 1) * LANES]
            for lvl in range(1, nd):
                step = dils[lvl] // dils[lvl - 1]
                n_rows = tm // dils[lvl]
                for r_prev in range(dils[lvl - 1]):
                    for r_sub in range(step):
                        r = r_prev + dils[lvl - 1] * r_sub
                        for j in range(nslab):
                            rows = stages[lvl - 1][r_prev, j, pl.ds(r_sub, n_rows, stride=step), :]
                            outs[lvl][0, r, :, j * LANES:(j + 1) * LANES] = rows.astype(BF16)
                            if lvl + 1 < nd:
                                stages[lvl][r, j] = rows


def _kvq(x, g_kv, g_q, sc, sh, w_kv, w_q, cos, sin, keep_rows, dils=(1,), tm=256):
    b, s, d = x.shape
    tm = min(tm, s)
    assert dils[0] == 1 and all(nxt % prv == 0 for prv, nxt in zip(dils, dils[1:]))
    n_skip = (s - keep_rows) // tm
    row = lambda bi, i: (bi, i, 0)
    per_row = sc.shape[1] == s and s > 1
    mod_spec = pl.BlockSpec((1, tm if per_row else 1, d), row if per_row else (lambda bi, i: (bi, 0, 0)))
    tab_spec = pl.BlockSpec((tm, LANES), lambda bi, i: (i, 0))
    kept = pl.BlockSpec((1, tm, d), lambda bi, i: (bi, jnp.maximum(i - n_skip, 0), 0))
    f32_kept = jax.ShapeDtypeStruct((b, keep_rows, d), F32)
    cls_specs = [pl.BlockSpec((1, dil, tm // dil, d), lambda bi, i: (bi, 0, i, 0)) for dil in dils]
    cls_shapes = [jax.ShapeDtypeStruct((b, dil, s // dil, d), BF16) for dil in dils]
    res = pl.pallas_call(
        functools.partial(_kvq_body, n_skip=n_skip, dils=tuple(dils)),
        grid=(b, s // tm),
        in_specs=[
            pl.BlockSpec((1, tm, d), row), _resident((1, d)), _resident((1, d)), mod_spec, mod_spec,
            _resident((d, 2 * d)), _resident((d, d)), tab_spec, tab_spec,
        ],
        out_specs=cls_specs * 3 + [kept, kept],
        out_shape=cls_shapes * 3 + [f32_kept, f32_kept],
        scratch_shapes=[pltpu.VMEM((dil, d // LANES, tm // dil, LANES), F32) for dil in dils[:-1]],
        compiler_params=_params(2),
        name="kvq_proj",
    )(x, g_kv.reshape(1, d), g_q.reshape(1, d), sc, sh, w_kv, w_q, cos, sin)
    nd = len(dils)
    return res[:nd], res[nd:2 * nd], res[2 * nd:3 * nd], res[3 * nd], res[3 * nd + 1]


def _dilated_body(q_ref, kp_ref, kc_ref, vp_ref, vc_ref, o_ref, lse_ref):
    i = pl.program_id(2)
    tq, dm = q_ref.shape
    a = lax.broadcasted_iota(jnp.int32, (tq, 2 * tq), 0)
    c = lax.broadcasted_iota(jnp.int32, (tq, 2 * tq), 1)
    keep = (c >= a) & (c <= a + tq) & ((c >= tq) | (i > 0))
    keep2 = jnp.concatenate([keep, keep], axis=0)
    lo_half = lax.broadcasted_iota(jnp.int32, (tq, PAIR), 1) < HEAD_DIM
    head_lane = lax.broadcasted_iota(jnp.int32, (tq, LANES), 1)
    q = q_ref[...]
    kcat = jnp.concatenate([kp_ref[...], kc_ref[...]], axis=0)
    vcat = jnp.concatenate([vp_ref[...], vc_ref[...]], axis=0)
    lse_all = jnp.zeros((tq, LANES), F32)
    for p in range(dm // PAIR):
        sl = slice(p * PAIR, (p + 1) * PAIR)
        qp, kp, vp = q[:, sl], kcat[:, sl], vcat[:, sl]
        zero = jnp.zeros_like(qp)
        q2 = jnp.concatenate([jnp.where(lo_half, qp, zero), jnp.where(lo_half, zero, qp)], axis=0)
        s = lax.dot_general(q2, kp, _NT, preferred_element_type=F32)
        s = jnp.where(keep2, s, NEG_INF)
        m = jnp.max(s, axis=1, keepdims=True)
        pr = jnp.exp2(s - m)
        l = jnp.sum(pr, axis=1, keepdims=True)
        pv = _dot(pr.astype(BF16), vp) / l
        lse = m + jnp.log2(l)
        o_ref[:, sl] = jnp.where(lo_half, pv[:tq], pv[tq:]).astype(BF16)
        lse_all = jnp.where(head_lane == 2 * p, lse[:tq], lse_all)
        lse_all = jnp.where(head_lane == 2 * p + 1, lse[tq:], lse_all)
    lse_ref[...] = lse_all


def _dilated_prompt(qs, ks, vs):
    tq = WINDOW_KEYS
    outs, lses = [], []
    for dil, q, k, v in zip(DILATIONS, qs, ks, vs):
        b, _, rows, d = q.shape
        cur = lambda bi, r, i: (bi, r, i, 0)
        prv = lambda bi, r, i: (bi, r, jnp.maximum(i - 1, 0), 0)
        wide = pl.BlockSpec((None, None, tq, d), cur)
        wide_prev = pl.BlockSpec((None, None, tq, d), prv)
        o_g, lse_g = pl.pallas_call(
            _dilated_body,
            grid=(b, dil, rows // tq),
            in_specs=[wide, wide_prev, wide, wide_prev, wide],
            out_specs=[wide, pl.BlockSpec((None, None, tq, LANES), cur)],
            out_shape=[jax.ShapeDtypeStruct((b, dil, rows, d), BF16),
                       jax.ShapeDtypeStruct((b, dil, rows, LANES), F32)],
            compiler_params=_params(3),
            name=f"dilated_prompt_d{dil}",
        )(q, k, k, v, v)
        outs.append(o_g)
        lses.append(lse_g)
    return outs, lses


def _dilated_sample_body(q_ref, kn_ref, vn_ref, k_ref, v_ref, o_ref, qb_ref, knb_ref, vnb_ref):
    j = pl.program_id(1)
    hg, rows = k_ref.shape[1], k_ref.shape[3]

    @pl.when(j == 0)
    def _():
        qb_ref[...] = _col_rep(q_ref[0])
        knb_ref[...] = _col_rep(kn_ref[0].astype(BF16))
        vnb_ref[...] = _col_rep(vn_ref[0].astype(BF16))

    idx = lax.broadcasted_iota(jnp.int32, (1, rows), 1)
    mult = jnp.zeros((1, rows), F32)
    for dil in DILATIONS:
        in_branch = (idx >= rows - WINDOW_KEYS * dil) & (lax.rem(rows - idx, dil) == 0)
        mult = mult + jnp.where(in_branch, 1.0, 0.0)
    valid = mult > 0.0
    n_self = float(len(DILATIONS))
    cols = []
    for h in range(hg):
        qh = qb_ref[j * hg + h]
        s = jnp.sum(k_ref[0, h] * jnp.tile(qh, (1, rows // LANES)), axis=0, keepdims=True)
        s_self = jnp.sum(knb_ref[j * hg + h] * qh, axis=0, keepdims=True)[:, 0:1]
        m = jnp.maximum(jnp.max(jnp.where(valid, s, NEG_INF), axis=1, keepdims=True), s_self)
        pr = jnp.where(valid, mult * jnp.exp2(s - m), 0.0)
        p_self = n_self * jnp.exp2(s_self - m)
        l = jnp.sum(pr, axis=1, keepdims=True) + p_self
        col = jnp.sum(v_ref[0, h] * pr, axis=1, keepdims=True) + p_self * vnb_ref[j * hg + h][:, 0:1]
        cols.append(col / l)
    o_ref[0] = _cols_to_rows(cols).astype(BF16)


def _dilated_sample(q, k_new, v_new, cache_kt, cache_vt, hg=16):
    n, d = q.shape
    _, nh, hd, rows = cache_kt.shape
    seq3 = lambda bi, j: (bi, 0, 0)
    slab = pl.BlockSpec((1, hg, hd, rows), lambda bi, j: (bi, j, 0, 0))
    row_spec = pl.BlockSpec((1, 1, d), seq3)
    rep = pltpu.VMEM((nh, hd, LANES), F32)
    out = pl.pallas_call(
        _dilated_sample_body,
        grid=(n, nh // hg),
        in_specs=[row_spec, row_spec, row_spec, slab, slab],
        out_specs=pl.BlockSpec((1, hg, hd), lambda bi, j: (bi, j, 0)),
        out_shape=jax.ShapeDtypeStruct((n, nh, hd), BF16),
        scratch_shapes=[rep, rep, rep],
        compiler_params=_params(2),
        name="dilated_sample",
    )(q.reshape(n, 1, d), k_new.reshape(n, 1, d), v_new.reshape(n, 1, d), cache_kt, cache_vt)
    return out.reshape(n, d)


def kernel(x_prompt, x_sample, cache_k_a, cache_v_a, cache_logf_a, page_table, cache_k_b, cache_v_b, state_conv, c_prompt, c_sample, w_mod, b_mod, g_norm, w_in_a, b_f_a, w_out_a, g_kv, w_kv_b, w_q_b, w_out_b, w_up, w_conv, b_conv, w_down):
    bp, sp, d = x_prompt.shape
    ns = x_sample.shape[0]
    depth = w_mod.shape[0]
    assert depth == 2 and x_sample.shape[1] == 1 and d == N_HEADS * HEAD_DIM and cache_k_a.shape[0] == 1
    n_pool, page = cache_k_a.shape[1], cache_k_a.shape[2]
    past_len = page_table.shape[1] * page
    wb = cache_k_b.shape[1]
    assert wb == W_MAX and sp >= W_MAX

    pad = (-(bp + ns)) % SUBLANES
    c_all = jnp.concatenate([c_prompt, c_sample, jnp.zeros((pad, d), F32)], axis=0)
    mod = _modulation(c_all, w_mod, b_mod).reshape(depth, bp + ns + pad, 6, d)
    mod_p = [[mod[l, :bp, j].reshape(bp, 1, d) for j in range(6)] for l in range(depth)]
    mod_s = [[mod[l, bp:bp + ns, j].reshape(1, ns, d) for j in range(6)] for l in range(depth)]

    w_qkv_a = w_in_a[0, :, :3 * d].astype(BF16)
    w_f_a = w_in_a[0, :, 3 * d:].astype(BF16)
    w_out_a16 = w_out_a[0].astype(BF16)
    w_kv16 = w_kv_b.astype(BF16)
    w_q16 = w_q_b[0].astype(BF16)
    w_out_b16 = w_out_b[0].astype(BF16)
    w_up16 = w_up.astype(BF16)
    w_down16 = w_down.astype(BF16)

    cos_p, sin_p = _rope_tables(jnp.arange(sp))
    cos_s, sin_s = _rope_tables(jnp.full((ns,), past_len, jnp.int32))

    cache_kt_a = jnp.transpose(cache_k_a.reshape(n_pool, page, N_HEADS, HEAD_DIM), (0, 2, 3, 1))
    cache_vt_a = jnp.transpose(cache_v_a.reshape(n_pool, page, N_HEADS, HEAD_DIM), (0, 2, 3, 1))
    cache_lft_a = jnp.transpose(cache_logf_a.reshape(n_pool, page, N_HEADS), (0, 2, 1))
    cache_kt_b = jnp.transpose(cache_k_b, (0, 2, 3, 1))
    cache_vt_b = jnp.transpose(cache_v_b, (0, 2, 3, 1))

    sh, sc, gt, shf, scf, gtf = mod_p[0]
    q_pm, k_pm, v_pm, ka_p, va_p, lfa_p, drow = _fox_proj_prompt(
        x_prompt, g_norm[0, 0], sc, sh, w_qkv_a, w_f_a, b_f_a[0])
    o_p = _fox_attn_prompt(q_pm, k_pm, v_pm, drow)
    h_p, cv_p0 = _post(x_prompt, o_p, w_out_a16, g_norm[0], gt, scf, shf, gtf,
                       w_up16[0], w_conv[0], b_conv[0], w_down16[0])

    sh_s, sc_s, gt_s, shf_s, scf_s, gtf_s = mod_s[0]
    xs = x_sample.reshape(ns, d)
    q_s, ka_s, va_s, lfa_s = _fox_proj_sample(
        xs, g_norm[0, 0], sc_s.reshape(ns, d), sh_s.reshape(ns, d), w_qkv_a, w_f_a, b_f_a[0])
    o_s = _fox_attn_sample(q_s, ka_s, va_s, lfa_s, cache_kt_a, cache_vt_a, cache_lft_a, page_table)
    prev0 = (state_conv[0][:, 0].reshape(1, ns, -1), state_conv[0][:, 1].reshape(1, ns, -1))
    h_s, a_s0 = _post(xs.reshape(1, ns, d), o_s.reshape(1, ns, d), w_out_a16, g_norm[0], gt_s, scf_s, shf_s, gtf_s,
                      w_up16[0], w_conv[0], b_conv[0], w_down16[0], prev=prev0)

    sh, sc, gt, shf, scf, gtf = mod_p[1]
    q_cls, k_cls, v_cls, kb_p, vb_p = _kvq(h_p, g_kv, g_norm[1, 0], sc, sh, w_kv16, w_q16, cos_p, sin_p,
                                           keep_rows=min(W_MAX, sp), dils=DILATIONS)
    sh_s, sc_s, gt_s, shf_s, scf_s, gtf_s = mod_s[1]
    (qb_s,), _, _, kb_s, vb_s = _kvq(h_s, g_kv, g_norm[1, 0], sc_s, sh_s, w_kv16, w_q16, cos_s, sin_s, keep_rows=ns)

    o1_p, lse_p = _dilated_prompt(q_cls, k_cls, v_cls)
    y_p, cv_p1 = _post(h_p, o1_p, w_out_b16, g_norm[1], gt, scf, shf, gtf,
                       w_up16[1], w_conv[1], b_conv[1], w_down16[1], lse=lse_p)
    o1_s = _dilated_sample(qb_s.reshape(ns, d), kb_s.reshape(ns, d), vb_s.reshape(ns, d), cache_kt_b, cache_vt_b)
    prev1 = (state_conv[1][:, 0].reshape(1, ns, -1), state_conv[1][:, 1].reshape(1, ns, -1))
    y_s, a_s1 = _post(h_s, o1_s.reshape(1, ns, d), w_out_b16, g_norm[1], gt_s, scf_s, shf_s, gtf_s,
                      w_up16[1], w_conv[1], b_conv[1], w_down16[1], prev=prev1)

    heads = (N_HEADS, HEAD_DIM)
    new_conv_sample = jnp.stack([
        jnp.stack([state_conv[0][:, 1], a_s0.reshape(ns, -1)], axis=1),
        jnp.stack([state_conv[1][:, 1], a_s1.reshape(ns, -1)], axis=1),
    ])
    return (
        y_p,
        y_s.reshape(ns, 1, d),
        ka_p.reshape(1, bp, sp, *heads),
        va_p.reshape(1, bp, sp, *heads),
        lfa_p.reshape(1, bp, sp, N_HEADS),
        ka_s.reshape(1, ns, 1, *heads),
        va_s.reshape(1, ns, 1, *heads),
        lfa_s.reshape(1, ns, 1, N_HEADS),
        kb_p.reshape(bp, -1, *heads),
        vb_p.reshape(bp, -1, *heads),
        kb_s.reshape(ns, 1, *heads),
        vb_s.reshape(ns, 1, *heads),
        jnp.stack([cv_p0, cv_p1]),
        new_conv_sample,
    )
```

```python
import functools

import jax
import jax.numpy as jnp
from jax import lax
from jax.experimental import pallas as pl
from jax.experimental.pallas import tpu as pltpu

F32 = jnp.float32
BF16 = jnp.bfloat16

N_HEADS = 16
HEAD_DIM = 64
DILATIONS = (1, 4, 16)
WINDOW_KEYS = 128
W_MAX = 2048
ROPE_THETA = 10000.0
NORM_EPS = 1e-6
NEG_INF = -1e30
SCALE = HEAD_DIM ** -0.5
LOG2_E = 1.4426950408889634

LANES = 128
SUBLANES = 8
PAIR = 2 * HEAD_DIM
VMEM_LIMIT_BYTES = 56 * 1024 * 1024
DOWN_PIECES = 2

_NT = (((1,), (1,)), ((), ()))
_TN = (((0,), (0,)), ((), ()))


def _params(n_grid_axes):
    return pltpu.CompilerParams(
        dimension_semantics=("arbitrary",) * n_grid_axes,
        vmem_limit_bytes=VMEM_LIMIT_BYTES,
    )


def _resident(shape):
    nd = len(shape)
    return pl.BlockSpec(shape, lambda *_: (0,) * nd, pipeline_mode=pl.Buffered(1))


def _rms_hat(x):
    return x * lax.rsqrt(jnp.mean(x * x, axis=-1, keepdims=True) + NORM_EPS)


def _log_sigmoid(x):
    return jnp.minimum(x, 0.0) - jnp.log1p(jnp.exp(-jnp.abs(x)))


def _gelu_tanh(x):
    return x * (0.5 * (1.0 + jnp.tanh(0.7978845608028654 * (x + 0.044715 * (x * x * x)))))


def _split3(x):
    hi = x.astype(BF16)
    r = x - hi.astype(F32)
    mid = r.astype(BF16)
    lo = (r - mid.astype(F32)).astype(BF16)
    return hi, mid, lo


def _dot(a, b):
    return jnp.dot(a, b, preferred_element_type=F32)


def _rope(x, cos, sin):
    d = x.shape[-1]
    lane = lax.broadcasted_iota(jnp.int32, x.shape, 1)
    first = (lane % HEAD_DIM) < (HEAD_DIM // 2)
    partner = jnp.where(first, pltpu.roll(x, d - HEAD_DIM // 2, 1), pltpu.roll(x, HEAD_DIM // 2, 1))
    reps = d // LANES
    return x * jnp.tile(cos, (1, reps)) + partner * jnp.tile(sin, (1, reps))


def _rope_tables(pos):
    half = HEAD_DIM // 2
    inv = jnp.power(ROPE_THETA, -jnp.arange(half, dtype=F32) * (2.0 / HEAD_DIM))
    ang = pos.astype(F32)[:, None] * inv[None, :]
    cos, sin = jnp.cos(ang), jnp.sin(ang)
    cos_h = jnp.concatenate([cos, cos], axis=-1)
    sin_h = jnp.concatenate([-sin, sin], axis=-1)
    return jnp.tile(cos_h, (1, LANES // HEAD_DIM)), jnp.tile(sin_h, (1, LANES // HEAD_DIM))


def _col_rep(row):
    d = row.shape[1]
    first_row = lax.broadcasted_iota(jnp.int32, (SUBLANES, LANES), 0) == 0
    e0 = jnp.where(first_row, 1.0, 0.0).astype(BF16)
    col = lax.dot_general(jnp.broadcast_to(row, (SUBLANES, d)), e0, _TN, preferred_element_type=F32)
    return col.reshape(d // HEAD_DIM, HEAD_DIM, LANES)


def _cols_to_rows(cols):
    lane = lax.broadcasted_iota(jnp.int32, (HEAD_DIM, LANES), 1)
    packed = jnp.zeros((HEAD_DIM, LANES), F32)
    for j, col in enumerate(cols):
        packed = jnp.where(lane == j, col, packed)
    eye = (lax.broadcasted_iota(jnp.int32, (HEAD_DIM, HEAD_DIM), 0)
           == lax.broadcasted_iota(jnp.int32, (HEAD_DIM, HEAD_DIM), 1))
    eye = jnp.where(eye, 1.0, 0.0).astype(BF16)
    rows = lax.dot_general(packed.astype(BF16), eye, _TN, preferred_element_type=F32)
    return rows[0:len(cols)]


def _mod_body(c_ref, w_ref, b_ref, o_ref):
    c = c_ref[...]
    s = (c * jax.nn.sigmoid(c)).astype(BF16)
    o_ref[0] = _dot(s, w_ref[0].astype(BF16)) + b_ref[0]


def _modulation(c_all, w_mod, b_mod):
    depth, d, n = w_mod.shape
    r = c_all.shape[0]
    tn = 1536
    return pl.pallas_call(
        _mod_body,
        grid=(depth, n // tn),
        in_specs=[
            pl.BlockSpec((r, d), lambda l, j: (0, 0)),
            pl.BlockSpec((1, d, tn), lambda l, j: (l, 0, j)),
            pl.BlockSpec((1, 1, tn), lambda l, j: (l, 0, j)),
        ],
        out_specs=pl.BlockSpec((1, r, tn), lambda l, j: (l, 0, j)),
        out_shape=jax.ShapeDtypeStruct((depth, r, n), F32),
        compiler_params=_params(2),
        name="modulation",
    )(c_all, w_mod, b_mod.reshape(depth, 1, n))


def _fox_proj_prompt_body(x_ref, g_ref, sc_ref, sh_ref, w_ref, wf_ref, wft_ref, bf_ref, bft_ref, tri_ref,
                          q_ref, k_ref, v_ref, kf_ref, vf_ref, lf_ref, d_ref, carry_ref):
    i = pl.program_id(1)
    x = x_ref[0]
    tm, dm = x.shape
    u = _rms_hat(x) * g_ref[...] * (1.0 + sc_ref[0]) + sh_ref[0]
    ub = u.astype(BF16)
    qkv = _dot(ub, w_ref[...])
    k = qkv[:, dm:2 * dm]
    v = qkv[:, 2 * dm:]
    kf_ref[0] = k
    vf_ref[0] = v
    qb = (qkv[:, :dm] * (SCALE * LOG2_E)).astype(BF16)
    kb = k.astype(BF16)
    vb = v.astype(BF16)
    for p in range(dm // PAIR):
        sl = slice(p * PAIR, (p + 1) * PAIR)
        q_ref[0, p] = qb[:, sl]
        k_ref[0, p] = kb[:, sl]
        v_ref[0, p] = vb[:, sl]
    lf_ref[0] = _log_sigmoid(_dot(ub, wf_ref[...]) + bf_ref[...])
    lft = _log_sigmoid(lax.dot_general(wft_ref[...], ub, _NT, preferred_element_type=F32) + bft_ref[...])
    tri = tri_ref[...]
    hi, mid, lo = _split3(lft)
    dloc = _dot(hi, tri) + _dot(mid, tri) + _dot(lo, tri)

    @pl.when(i == 0)
    def _():
        carry_ref[...] = jnp.zeros_like(carry_ref)

    dfull = dloc + carry_ref[...]
    carry_ref[...] = dfull[:, tm - 1:tm]
    for h in range(N_HEADS):
        d_ref[0, h] = dfull[h:h + 1, :]


def _fox_proj_prompt(x, g, sc, sh, w_qkv, w_f, b_f, tm=512):
    b, s, d = x.shape
    npair = d // PAIR
    nh = w_f.shape[1]
    tri = (jnp.arange(tm)[:, None] <= jnp.arange(tm)[None, :]).astype(BF16)
    pm_shape = jax.ShapeDtypeStruct((b, npair, s, PAIR), BF16)
    pm_spec = pl.BlockSpec((1, npair, tm, PAIR), lambda bi, i: (bi, 0, i, 0))
    row_spec = pl.BlockSpec((1, tm, d), lambda bi, i: (bi, i, 0))
    mod_spec = pl.BlockSpec((1, 1, d), lambda bi, i: (bi, 0, 0))
    return pl.pallas_call(
        _fox_proj_prompt_body,
        grid=(b, s // tm),
        in_specs=[
            row_spec, _resident((1, d)), mod_spec, mod_spec,
            _resident((d, 3 * d)), _resident((d, nh)), _resident((nh, d)),
            _resident((1, nh)), _resident((nh, 1)), _resident((tm, tm)),
        ],
        out_specs=[
            pm_spec, pm_spec, pm_spec, row_spec, row_spec,
            pl.BlockSpec((1, tm, nh), lambda bi, i: (bi, i, 0)),
            pl.BlockSpec((1, nh, 1, tm), lambda bi, i: (bi, 0, 0, i)),
        ],
        out_shape=[
            pm_shape, pm_shape, pm_shape,
            jax.ShapeDtypeStruct((b, s, d), F32), jax.ShapeDtypeStruct((b, s, d), F32),
            jax.ShapeDtypeStruct((b, s, nh), F32), jax.ShapeDtypeStruct((b, nh, 1, s), F32),
        ],
        scratch_shapes=[pltpu.VMEM((nh, 1), F32)],
        compiler_params=_params(2),
        name="fox_proj_prompt",
    )(x, g.reshape(1, d), sc, sh, w_qkv, w_f, w_f.T, b_f.reshape(1, nh), b_f.reshape(nh, 1), tri)


def _fox_proj_sample_body(x_ref, g_ref, sc_ref, sh_ref, w_ref, wf_ref, bf_ref, q_ref, k_ref, v_ref, lf_ref):
    x = x_ref[...]
    dm = x.shape[1]
    u = _rms_hat(x) * g_ref[...] * (1.0 + sc_ref[...]) + sh_ref[...]
    ub = u.astype(BF16)
    qkv = _dot(ub, w_ref[...])
    q_ref[...] = (qkv[:, :dm] * SCALE).astype(BF16)
    k_ref[...] = qkv[:, dm:2 * dm]
    v_ref[...] = qkv[:, 2 * dm:]
    lf_ref[...] = _log_sigmoid(_dot(ub, wf_ref[...]) + bf_ref[...])


def _fox_proj_sample(x, g, sc, sh, w_qkv, w_f, b_f):
    n, d = x.shape
    nh = w_f.shape[1]
    return pl.pallas_call(
        _fox_proj_sample_body,
        out_shape=[
            jax.ShapeDtypeStruct((n, d), BF16), jax.ShapeDtypeStruct((n, d), F32),
            jax.ShapeDtypeStruct((n, d), F32), jax.ShapeDtypeStruct((n, nh), F32),
        ],
        compiler_params=pltpu.CompilerParams(vmem_limit_bytes=VMEM_LIMIT_BYTES),
        name="fox_proj_sample",
    )(x, g.reshape(1, d), sc, sh, w_qkv, w_f, b_f.reshape(1, nh))


def _fox_attn_body(qi_ref, kj_ref, q_ref, k_ref, v_ref, dk_ref, dq_ref, o_ref, m_ref, acc_ref):
    t = pl.program_id(1)
    qi = qi_ref[t]
    kj = kj_ref[t]
    npair, tq = q_ref.shape[1], q_ref.shape[2]
    tk = k_ref.shape[2]
    lo_half = lax.broadcasted_iota(jnp.int32, (tq, PAIR), 1) < HEAD_DIM
    lo_half_k = lax.broadcasted_iota(jnp.int32, (tk, PAIR), 1) < HEAD_DIM

    @pl.when(kj == 0)
    def _():
        m_ref[...] = jnp.full(m_ref.shape, NEG_INF, F32)
        acc_ref[...] = jnp.zeros_like(acc_ref)

    def run(masked):
        if masked:
            keep = (lax.broadcasted_iota(jnp.int32, (tq, tk), 1)
                    <= lax.broadcasted_iota(jnp.int32, (tq, tk), 0))

        def pair_body(p, carry):
            qp = q_ref[0, p]
            kp = k_ref[0, p]
            vp = v_ref[0, p]
            for hh in range(2):
                h = 2 * p + hh
                sel = lo_half if hh == 0 else jnp.logical_not(lo_half)
                sel_k = lo_half_k if hh == 0 else jnp.logical_not(lo_half_k)
                qh = jnp.where(sel, qp, jnp.zeros_like(qp))
                s = lax.dot_general(qh, kp, _NT, preferred_element_type=F32)
                s = s + (dq_ref[0, h][:, 0:1] - dk_ref[0, h]) * LOG2_E
                if masked:
                    s = jnp.where(keep, s, NEG_INF)
                m_prev = m_ref[h]
                m_new = jnp.maximum(m_prev, jnp.max(s, axis=1, keepdims=True))
                alpha = jnp.exp2(m_prev - m_new)
                pr = jnp.exp2(s - jnp.tile(m_new, (1, tk // LANES)))
                m_ref[h] = m_new
                vh = jnp.where(sel_k, vp, jnp.ones_like(vp))
                acc_ref[h] = acc_ref[h] * alpha + _dot(pr.astype(BF16), vh)
            return carry

        lax.fori_loop(0, npair, pair_body, 0, unroll=4)

    @pl.when(kj < qi)
    def _():
        run(False)

    @pl.when(kj == qi)
    def _():
        run(True)
        for p in range(npair):
            a0 = acc_ref[2 * p]
            a1 = acc_ref[2 * p + 1]
            num = jnp.where(lo_half, a0, a1)
            den = pltpu.roll(jnp.where(lo_half, a1, a0), HEAD_DIM, 1)
            o_ref[0, :, p * PAIR:(p + 1) * PAIR] = (num / den).astype(BF16)


def _fox_attn_prompt(q_pm, k_pm, v_pm, drow, t=512):
    b, npair, s, _ = q_pm.shape
    nh = drow.shape[1]
    nt = s // t
    pairs = [(i, j) for i in range(nt) for j in range(i + 1)]
    qi = jnp.asarray([p[0] for p in pairs], jnp.int32)
    kj = jnp.asarray([p[1] for p in pairs], jnp.int32)
    grid_spec = pltpu.PrefetchScalarGridSpec(
        num_scalar_prefetch=2,
        grid=(b, len(pairs)),
        in_specs=[
            pl.BlockSpec((1, npair, t, PAIR), lambda bi, n, qi, kj: (bi, 0, qi[n], 0)),
            pl.BlockSpec((1, npair, t, PAIR), lambda bi, n, qi, kj: (bi, 0, kj[n], 0)),
            pl.BlockSpec((1, npair, t, PAIR), lambda bi, n, qi, kj: (bi, 0, kj[n], 0)),
            pl.BlockSpec((1, nh, 1, t), lambda bi, n, qi, kj: (bi, 0, 0, kj[n])),
            pl.BlockSpec((1, nh, 1, LANES), lambda bi, n, qi, kj: (bi, 0, 0, qi[n] * (t // LANES))),
        ],
        out_specs=pl.BlockSpec((1, t, npair * PAIR), lambda bi, n, qi, kj: (bi, qi[n], 0)),
        scratch_shapes=[pltpu.VMEM((nh, t, LANES), F32), pltpu.VMEM((nh, t, PAIR), F32)],
    )
    return pl.pallas_call(
        _fox_attn_body,
        grid_spec=grid_spec,
        out_shape=jax.ShapeDtypeStruct((b, s, npair * PAIR), BF16),
        compiler_params=_params(2),
        name="fox_attn_prompt",
    )(qi, kj, q_pm, k_pm, v_pm, drow, drow)


def _fox_decode_body(pt_ref, q_ref, kn_ref, vn_ref, lfn_ref, *refs, pages):
    k_refs = refs[:pages]
    v_refs = refs[pages:2 * pages]
    lf_refs = refs[2 * pages:3 * pages]
    o_ref, qb_ref, m_ref, l_ref, acc_ref, carry_ref, s_scr = refs[3 * pages:]
    g = pl.program_id(1)
    nh, page = lf_refs[0].shape[1], lf_refs[0].shape[2]
    lane = lax.broadcasted_iota(jnp.int32, (nh, page), 1)

    @pl.when(g == 0)
    def _():
        qb_ref[...] = _col_rep(q_ref[0])
        m_ref[...] = jnp.full(m_ref.shape, NEG_INF, F32)
        l_ref[...] = jnp.zeros_like(l_ref)
        acc_ref[...] = jnp.zeros_like(acc_ref)
        carry_ref[...] = jnp.zeros_like(carry_ref)

    for u in reversed(range(pages)):
        lft = lf_refs[u][0]
        suf = lft
        shift = 1
        while shift < page:
            suf = suf + jnp.where(lane + shift < page, pltpu.roll(suf, page - shift, 1), 0.0)
            shift *= 2
        carry = carry_ref[...]
        s_scr[:, u * page:(u + 1) * page] = (suf - lft) + carry
        carry_ref[...] = carry + suf[:, 0:1]
    for h in range(nh):
        qh = qb_ref[h]
        for u in range(pages):
            sl = slice(u * page, (u + 1) * page)
            s_scr[h:h + 1, sl] = s_scr[h:h + 1, sl] + jnp.sum(k_refs[u][0, h] * qh, axis=0, keepdims=True)
    s = s_scr[...]
    m_prev = m_ref[...]
    m_new = jnp.maximum(m_prev, jnp.max(s, axis=1, keepdims=True))
    alpha = jnp.exp(m_prev - m_new)
    pr = jnp.exp(s - m_new)
    l_ref[...] = alpha * l_ref[...] + jnp.sum(pr, axis=1, keepdims=True)
    m_ref[...] = m_new
    for h in range(nh):
        acc = acc_ref[h] * alpha[h:h + 1, :]
        for u in range(pages):
            acc = acc + pr[h:h + 1, u * page:(u + 1) * page] * v_refs[u][0, h]
        acc_ref[h] = acc

    @pl.when(g == pl.num_programs(1) - 1)
    def _():
        knb = _col_rep(kn_ref[0].astype(BF16))
        vnb = _col_rep(vn_ref[0].astype(BF16))
        for h in range(nh):
            s_scr[h:h + 1, 0:page] = jnp.sum(knb[h] * qb_ref[h], axis=0, keepdims=True)
        s_new = jnp.where(lane == 0, s_scr[:, 0:page] - lfn_ref[0], NEG_INF)
        m_old = m_ref[...]
        m_fin = jnp.maximum(m_old, jnp.max(s_new, axis=1, keepdims=True))
        a_fin = jnp.exp(m_old - m_fin)
        p_new = jnp.exp(s_new - m_fin)
        l_fin = a_fin * l_ref[...] + jnp.sum(p_new, axis=1, keepdims=True)
        cols = []
        for h in range(nh):
            acc = acc_ref[h] * a_fin[h:h + 1, :] + p_new[h:h + 1, :] * vnb[h]
            cols.append(jnp.sum(acc, axis=1, keepdims=True) / l_fin[h:h + 1, :])
        o_ref[0] = _cols_to_rows(cols).astype(BF16)


def _fox_attn_sample(q, k_new, v_new, lf_new, cache_kt, cache_vt, cache_lft, page_table, pages=16):
    n, d = q.shape
    _, nh, hd, page = cache_kt.shape
    n_groups = page_table.shape[1] // pages

    def page_map(nd, u):
        return lambda bi, g, pt: (pt[bi, (n_groups - 1 - g) * pages + u],) + (0,) * (nd - 1)

    seq3 = lambda bi, g, pt: (bi, 0, 0)
    kv_specs = [pl.BlockSpec((1, nh, hd, page), page_map(4, u)) for u in range(pages)]
    lf_specs = [pl.BlockSpec((1, nh, page), page_map(3, u)) for u in range(pages)]
    grid_spec = pltpu.PrefetchScalarGridSpec(
        num_scalar_prefetch=1,
        grid=(n, n_groups),
        in_specs=[
            pl.BlockSpec((1, 1, d), seq3), pl.BlockSpec((1, 1, d), seq3), pl.BlockSpec((1, 1, d), seq3),
            pl.BlockSpec((1, nh, 1), seq3),
        ] + kv_specs + kv_specs + lf_specs,
        out_specs=pl.BlockSpec((1, nh, hd), seq3),
        scratch_shapes=[
            pltpu.VMEM((nh, hd, LANES), F32), pltpu.VMEM((nh, 1), F32), pltpu.VMEM((nh, 1), F32),
            pltpu.VMEM((nh, hd, LANES), F32), pltpu.VMEM((nh, 1), F32), pltpu.VMEM((nh, pages * page), F32),
        ],
    )
    out = pl.pallas_call(
        functools.partial(_fox_decode_body, pages=pages),
        grid_spec=grid_spec,
        out_shape=jax.ShapeDtypeStruct((n, nh, hd), BF16),
        compiler_params=_params(2),
        name="fox_attn_sample",
    )(page_table, q.reshape(n, 1, d), k_new.reshape(n, 1, d), v_new.reshape(n, 1, d), lf_new.reshape(n, nh, 1),
      *([cache_kt] * pages), *([cache_vt] * pages), *([cache_lft] * pages))
    return out.reshape(n, d)


def _post_body(*refs, prompt, chunk, n_br):
    x_ref = refs[0]
    o_refs = refs[1:1 + n_br]
    refs = refs[1 + n_br:]
    if n_br > 1:
        lse_refs = refs[:n_br]
        e_ref = refs[n_br]
        refs = refs[n_br + 1:]
    wo_ref, g_ref, gt_ref, scf_ref, shf_ref, gtf_ref, wup_ref, wc_ref, bc_ref, wdn_ref = refs[:10]
    refs = refs[10:]
    if prompt:
        h_ref, cv_ref, a_scr, act_scr = refs[:4]
    else:
        p0_ref, p1_ref, h_ref, cv_ref, act_scr = refs[:5]
    x = x_ref[0]
    tm, dm = x.shape
    dff = wdn_ref.shape[0]
    g = g_ref[...]
    if n_br == 1:
        o = o_refs[0][0]
    else:
        nat_scr, lse_scr = refs[-2:]
        nslab = dm // LANES
        outs, lses = [], []
        ci = 0
        for dil, o_ref, lse_ref in zip(DILATIONS, o_refs, lse_refs):
            if dil == 1:
                outs.append(o_ref[0, 0].astype(F32))
                lses.append(lse_ref[0, 0])
                continue
            for r in range(dil):
                lse_scr[ci, pl.ds(r, tm // dil, stride=dil), :] = lse_ref[0, r]
                for j in range(nslab):
                    rows = o_ref[0, r, :, j * LANES:(j + 1) * LANES].astype(F32)
                    nat_scr[ci, j, pl.ds(r, tm // dil, stride=dil), :] = rows
            outs.append(jnp.concatenate([nat_scr[ci, j] for j in range(nslab)], axis=1))
            lses.append(lse_scr[ci])
            ci += 1
        top = functools.reduce(jnp.maximum, lses)
        es = [jnp.exp2(l - top) for l in lses]
        den = functools.reduce(lambda a, b: a + b, es)
        o = None
        for e, o_g in zip(es, outs):
            term = _dot((e / den).astype(BF16), e_ref[...]) * o_g
            o = term if o is None else o + term
        o = o.astype(BF16)
    att = _dot(o, wo_ref[...])
    h1 = x + gt_ref[0] * (_rms_hat(att) * g[1:2])
    u = (_rms_hat(h1) * g[2:3] * (1.0 + scf_ref[0]) + shf_ref[0]).astype(BF16)
    if prompt:
        i = pl.program_id(1)
        halo = a_scr.shape[0] - tm

        @pl.when(i == 0)
        def _():
            a_scr[0:halo, :] = jnp.zeros((halo, a_scr.shape[1]), F32)

    n_chunks = dff // chunk
    flush_at = [((k + 1) * n_chunks) // DOWN_PIECES - 1 for k in range(DOWN_PIECES)]
    ffn = None
    start = 0
    for c in range(n_chunks):
        ys = []
        for base in (0, dff):
            sl = slice(base + c * chunk, base + (c + 1) * chunk)
            a = _dot(u, wup_ref[:, sl])
            if prompt:
                a_scr[halo:halo + tm, sl] = a
                am2 = a_scr[halo - 2:halo - 2 + tm, sl]
                am1 = a_scr[halo - 1:halo - 1 + tm, sl]
            else:
                cv_ref[0, :, sl] = a
                am2 = p0_ref[0, :, sl]
                am1 = p1_ref[0, :, sl]
            ys.append(bc_ref[:, sl] + wc_ref[0:1, sl] * am2 + wc_ref[1:2, sl] * am1 + wc_ref[2:3, sl] * a)
        act_scr[:, c * chunk:(c + 1) * chunk] = (_gelu_tanh(ys[0]) * ys[1]).astype(BF16)
        if c in flush_at:
            stop = (c + 1) * chunk
            part = _dot(act_scr[:, start:stop], wdn_ref[start:stop, :])
            ffn = part if ffn is None else ffn + part
            start = stop
    if prompt:
        @pl.when(i == pl.num_programs(1) - 1)
        def _():
            cv_ref[0] = a_scr[halo + tm - 2:halo + tm, :]

        a_scr[0:halo, :] = a_scr[tm:tm + halo, :]
    h_ref[0] = h1 + gtf_ref[0] * (_rms_hat(ffn) * g[3:4])


def _post_rows(s, d, dff, n_br, prompt):
    resident = 2 * (d * d + d * 2 * dff + dff * d) + 4 * (4 * d + 4 * 2 * dff)
    n_cls = sum(1 for dil in DILATIONS if dil > 1) if n_br > 1 else 0
    for tm in (512, 256, 128, 64, 32, 16, 8):
        if s % tm:
            continue
        tiles = 2 * (tm * d * 4) * 2 + 2 * n_br * tm * d * 2
        scratch = tm * dff * 2 + ((tm + SUBLANES) * 2 * dff * 4 if prompt else 4 * tm * 2 * dff * 4)
        mix = (2 * n_br * tm * LANES * 4 + n_cls * (tm * d * 4 + tm * LANES * 4)) if n_br > 1 else 0
        live = 4 * tm * d * 4
        if resident + tiles + scratch + mix + live <= VMEM_LIMIT_BYTES:
            return tm
    raise ValueError("post kernel does not fit VMEM")


def _post(x, o, w_out, g4, gt, scf, shf, gtf, w_up, w_conv, b_conv, w_down, lse=None, prev=None, chunk=256):
    b, s, d = x.shape
    dff = w_down.shape[0]
    prompt = prev is None
    o_list = list(o) if isinstance(o, (list, tuple)) else [o]
    n_br = len(o_list)
    tm = _post_rows(s, d, dff, n_br, prompt)
    halo = SUBLANES
    row = lambda bi, i: (bi, i, 0)

    def mod_spec(a):
        per_row = a.shape[1] == s and s > 1
        return pl.BlockSpec((1, tm if per_row else 1, d), row if per_row else (lambda bi, i: (bi, 0, 0)))

    in_specs = [pl.BlockSpec((1, tm, d), row)]
    args = [x] + o_list
    scratch_mix = []
    if n_br == 1:
        in_specs += [pl.BlockSpec((1, tm, d), row)]
    else:
        head_of_lane = jnp.arange(d)[None, :] // HEAD_DIM
        expand = (jnp.arange(LANES)[:, None] == head_of_lane).astype(BF16)
        cls = lambda bi, i: (bi, 0, i, 0)
        in_specs += [pl.BlockSpec((1, dil, tm // dil, d), cls) for dil in DILATIONS]
        in_specs += [pl.BlockSpec((1, dil, tm // dil, LANES), cls) for dil in DILATIONS]
        in_specs += [_resident((LANES, d))]
        args += list(lse) + [expand]
        n_cls = sum(1 for dil in DILATIONS if dil > 1)
        scratch_mix = [pltpu.VMEM((n_cls, d // LANES, tm, LANES), F32), pltpu.VMEM((n_cls, tm, LANES), F32)]
    in_specs += [
        _resident((d, d)), _resident((4, d)), mod_spec(gt), mod_spec(scf), mod_spec(shf), mod_spec(gtf),
        _resident((d, 2 * dff)), _resident((3, 2 * dff)), _resident((1, 2 * dff)), _resident((dff, d)),
    ]
    args += [w_out, g4, gt, scf, shf, gtf, w_up, w_conv, b_conv.reshape(1, 2 * dff), w_down]
    scratch = [pltpu.VMEM((tm, dff), BF16)]
    if prompt:
        cv_shape = jax.ShapeDtypeStruct((b, 2, 2 * dff), F32)
        cv_spec = pl.BlockSpec((1, 2, 2 * dff), lambda bi, i: (bi, 0, 0))
        scratch = [pltpu.VMEM((tm + halo, 2 * dff), F32)] + scratch
    else:
        cv_shape = jax.ShapeDtypeStruct((b, s, 2 * dff), F32)
        cv_spec = pl.BlockSpec((1, tm, 2 * dff), row)
        in_specs += [pl.BlockSpec((1, tm, 2 * dff), row)] * 2
        args += [prev[0], prev[1]]
    return pl.pallas_call(
        functools.partial(_post_body, prompt=prompt, chunk=chunk, n_br=n_br),
        grid=(b, s // tm),
        in_specs=in_specs,
        out_specs=[pl.BlockSpec((1, tm, d), row), cv_spec],
        out_shape=[jax.ShapeDtypeStruct((b, s, d), F32), cv_shape],
        scratch_shapes=scratch + scratch_mix,
        compiler_params=_params(2),
        name="post_prompt" if prompt else "post_sample",
    )(*args)


def _kvq_body(x_ref, gkv_ref, gq_ref, sc_ref, sh_ref, wkv_ref, wq_ref, cos_ref, sin_ref, *refs, n_skip, dils):
    nd = len(dils)
    q_refs, k_refs, v_refs = refs[:nd], refs[nd:2 * nd], refs[2 * nd:3 * nd]
    kf_ref, vf_ref = refs[3 * nd:3 * nd + 2]
    i = pl.program_id(1)
    x = x_ref[0]
    tm, dm = x.shape
    xh = _rms_hat(x)
    cos = cos_ref[...]
    sin = sin_ref[...]
    kv = _dot((xh * gkv_ref[...]).astype(BF16), wkv_ref[...])
    k = _rope(kv[:, :dm], cos, sin)
    v = kv[:, dm:]
    uq = (xh * gq_ref[...] * (1.0 + sc_ref[0]) + sh_ref[0]).astype(BF16)
    q = _rope(_dot(uq, wq_ref[...]), cos, sin) * (SCALE * LOG2_E)

    @pl.when(i >= n_skip)
    def _():
        kf_ref[0] = k
        vf_ref[0] = v

    nslab = dm // LANES
    for val, outs in ((q, q_refs), (k, k_refs), (v, v_refs)):
        outs[0][0, 0] = val.astype(BF16)
        if nd > 1:
            stages = refs[3 * nd + 2:]
            for j in range(nslab):
                stages[0][0, j] = val[:, j * LANES:(j + 1) * LANES]
            for lvl in range(1, nd):
                step = dils[lvl] // dils[lvl - 1]
                n_rows = tm // dils[lvl]
                for r_prev in range(dils[lvl - 1]):
                    for r_sub in range(step):
                        r = r_prev + dils[lvl - 1] * r_sub
                        for j in range(nslab):
                            rows = stages[lvl - 1][r_prev, j, pl.ds(r_sub, n_rows, stride=step), :]
                            outs[lvl][0, r, :, j * LANES:(j + 1) * LANES] = rows.astype(BF16)
                            if lvl + 1 < nd:
                                stages[lvl][r, j] = rows


def _kvq(x, g_kv, g_q, sc, sh, w_kv, w_q, cos, sin, keep_rows, dils=(1,), tm=256):
    b, s, d = x.shape
    tm = min(tm, s)
    assert dils[0] == 1 and all(nxt % prv == 0 for prv, nxt in zip(dils, dils[1:]))
    n_skip = (s - keep_rows) // tm
    row = lambda bi, i: (bi, i, 0)
    per_row = sc.shape[1] == s and s > 1
    mod_spec = pl.BlockSpec((1, tm if per_row else 1, d), row if per_row else (lambda bi, i: (bi, 0, 0)))
    tab_spec = pl.BlockSpec((tm, LANES), lambda bi, i: (i, 0))
    kept = pl.BlockSpec((1, tm, d), lambda bi, i: (bi, jnp.maximum(i - n_skip, 0), 0))
    f32_kept = jax.ShapeDtypeStruct((b, keep_rows, d), F32)
    cls_specs = [pl.BlockSpec((1, dil, tm // dil, d), lambda bi, i: (bi, 0, i, 0)) for dil in dils]
    cls_shapes = [jax.ShapeDtypeStruct((b, dil, s // dil, d), BF16) for dil in dils]
    res = pl.pallas_call(
        functools.partial(_kvq_body, n_skip=n_skip, dils=tuple(dils)),
        grid=(b, s // tm),
        in_specs=[
            pl.BlockSpec((1, tm, d), row), _resident((1, d)), _resident((1, d)), mod_spec, mod_spec,
            _resident((d, 2 * d)), _resident((d, d)), tab_spec, tab_spec,
        ],
        out_specs=cls_specs * 3 + [kept, kept],
        out_shape=cls_shapes * 3 + [f32_kept, f32_kept],
        scratch_shapes=[pltpu.VMEM((dil, d // LANES, tm // dil, LANES), F32) for dil in dils[:-1]],
        compiler_params=_params(2),
        name="kvq_proj",
    )(x, g_kv.reshape(1, d), g_q.reshape(1, d), sc, sh, w_kv, w_q, cos, sin)
    nd = len(dils)
    return res[:nd], res[nd:2 * nd], res[2 * nd:3 * nd], res[3 * nd], res[3 * nd + 1]


def _dilated_body(q_ref, kp_ref, kc_ref, vp_ref, vc_ref, o_ref, lse_ref):
    i = pl.program_id(2)
    tq, dm = q_ref.shape
    a = lax.broadcasted_iota(jnp.int32, (tq, 2 * tq), 0)
    c = lax.broadcasted_iota(jnp.int32, (tq, 2 * tq), 1)
    keep = (c >= a) & (c <= a + tq) & ((c >= tq) | (i > 0))
    keep2 = jnp.concatenate([keep, keep], axis=0)
    lo_half = lax.broadcasted_iota(jnp.int32, (tq, PAIR), 1) < HEAD_DIM
    q = q_ref[...]
    kcat = jnp.concatenate([kp_ref[...], kc_ref[...]], axis=0)
    vcat = jnp.concatenate([vp_ref[...], vc_ref[...]], axis=0)
    lse_ref[...] = jnp.zeros((tq, LANES), F32)
    for p in range(dm // PAIR):
        sl = slice(p * PAIR, (p + 1) * PAIR)
        qp, kp, vp = q[:, sl], kcat[:, sl], vcat[:, sl]
        zero = jnp.zeros_like(qp)
        q2 = jnp.concatenate([jnp.where(lo_half, qp, zero), jnp.where(lo_half, zero, qp)], axis=0)
        s = lax.dot_general(q2, kp, _NT, preferred_element_type=F32)
        s = jnp.where(keep2, s, NEG_INF)
        m = jnp.max(s, axis=1, keepdims=True)
        pr = jnp.exp2(s - m)
        l = jnp.sum(pr, axis=1, keepdims=True)
        pv = _dot(pr.astype(BF16), vp) / l
        lse = m + jnp.log2(l)
        o_ref[:, sl] = jnp.where(lo_half, pv[:tq], pv[tq:]).astype(BF16)
        lse_ref[:, 2 * p:2 * p + 1] = lse[:tq]
        lse_ref[:, 2 * p + 1:2 * p + 2] = lse[tq:]


def _dilated_prompt(qs, ks, vs):
    tq = WINDOW_KEYS
    outs, lses = [], []
    for dil, q, k, v in zip(DILATIONS, qs, ks, vs):
        b, _, rows, d = q.shape
        cur = lambda bi, r, i: (bi, r, i, 0)
        prv = lambda bi, r, i: (bi, r, jnp.maximum(i - 1, 0), 0)
        wide = pl.BlockSpec((None, None, tq, d), cur)
        wide_prev = pl.BlockSpec((None, None, tq, d), prv)
        o_g, lse_g = pl.pallas_call(
            _dilated_body,
            grid=(b, dil, rows // tq),
            in_specs=[wide, wide_prev, wide, wide_prev, wide],
            out_specs=[wide, pl.BlockSpec((None, None, tq, LANES), cur)],
            out_shape=[jax.ShapeDtypeStruct((b, dil, rows, d), BF16),
                       jax.ShapeDtypeStruct((b, dil, rows, LANES), F32)],
            compiler_params=_params(3),
            name=f"dilated_prompt_d{dil}",
        )(q, k, k, v, v)
        outs.append(o_g)
        lses.append(lse_g)
    return outs, lses


def _dilated_sample_body(q_ref, kn_ref, vn_ref, k_ref, v_ref, o_ref, qb_ref, knb_ref, vnb_ref):
    j = pl.program_id(1)
    hg, rows = k_ref.shape[1], k_ref.shape[3]

    @pl.when(j == 0)
    def _():
        qb_ref[...] = _col_rep(q_ref[0])
        knb_ref[...] = _col_rep(kn_ref[0].astype(BF16))
        vnb_ref[...] = _col_rep(vn_ref[0].astype(BF16))

    idx = lax.broadcasted_iota(jnp.int32, (1, rows), 1)
    mult = jnp.zeros((1, rows), F32)
    for dil in DILATIONS:
        in_branch = (idx >= rows - WINDOW_KEYS * dil) & (lax.rem(rows - idx, dil) == 0)
        mult = mult + jnp.where(in_branch, 1.0, 0.0)
    valid = mult > 0.0
    n_self = float(len(DILATIONS))
    cols = []
    for h in range(hg):
        qh = qb_ref[j * hg + h]
        s = jnp.sum(k_ref[0, h] * jnp.tile(qh, (1, rows // LANES)), axis=0, keepdims=True)
        s_self = jnp.sum(knb_ref[j * hg + h] * qh, axis=0, keepdims=True)[:, 0:1]
        m = jnp.maximum(jnp.max(jnp.where(valid, s, NEG_INF), axis=1, keepdims=True), s_self)
        pr = jnp.where(valid, mult * jnp.exp2(s - m), 0.0)
        p_self = n_self * jnp.exp2(s_self - m)
        l = jnp.sum(pr, axis=1, keepdims=True) + p_self
        col = jnp.sum(v_ref[0, h] * pr, axis=1, keepdims=True) + p_self * vnb_ref[j * hg + h][:, 0:1]
        cols.append(col / l)
    o_ref[0] = _cols_to_rows(cols).astype(BF16)


def _dilated_sample(q, k_new, v_new, cache_kt, cache_vt, hg=16):
    n, d = q.shape
    _, nh, hd, rows = cache_kt.shape
    seq3 = lambda bi, j: (bi, 0, 0)
    slab = pl.BlockSpec((1, hg, hd, rows), lambda bi, j: (bi, j, 0, 0))
    row_spec = pl.BlockSpec((1, 1, d), seq3)
    rep = pltpu.VMEM((nh, hd, LANES), F32)
    out = pl.pallas_call(
        _dilated_sample_body,
        grid=(n, nh // hg),
        in_specs=[row_spec, row_spec, row_spec, slab, slab],
        out_specs=pl.BlockSpec((1, hg, hd), lambda bi, j: (bi, j, 0)),
        out_shape=jax.ShapeDtypeStruct((n, nh, hd), BF16),
        scratch_shapes=[rep, rep, rep],
        compiler_params=_params(2),
        name="dilated_sample",
    )(q.reshape(n, 1, d), k_new.reshape(n, 1, d), v_new.reshape(n, 1, d), cache_kt, cache_vt)
    return out.reshape(n, d)


def kernel(x_prompt, x_sample, cache_k_a, cache_v_a, cache_logf_a, page_table, cache_k_b, cache_v_b, state_conv, c_prompt, c_sample, w_mod, b_mod, g_norm, w_in_a, b_f_a, w_out_a, g_kv, w_kv_b, w_q_b, w_out_b, w_up, w_conv, b_conv, w_down):
    bp, sp, d = x_prompt.shape
    ns = x_sample.shape[0]
    depth = w_mod.shape[0]
    assert depth == 2 and x_sample.shape[1] == 1 and d == N_HEADS * HEAD_DIM and cache_k_a.shape[0] == 1
    n_pool, page = cache_k_a.shape[1], cache_k_a.shape[2]
    past_len = page_table.shape[1] * page
    wb = cache_k_b.shape[1]
    assert wb == W_MAX and sp >= W_MAX

    pad = (-(bp + ns)) % SUBLANES
    c_all = jnp.concatenate([c_prompt, c_sample, jnp.zeros((pad, d), F32)], axis=0)
    mod = _modulation(c_all, w_mod, b_mod).reshape(depth, bp + ns + pad, 6, d)
    mod_p = [[mod[l, :bp, j].reshape(bp, 1, d) for j in range(6)] for l in range(depth)]
    mod_s = [[mod[l, bp:bp + ns, j].reshape(1, ns, d) for j in range(6)] for l in range(depth)]

    w_qkv_a = w_in_a[0, :, :3 * d].astype(BF16)
    w_f_a = w_in_a[0, :, 3 * d:].astype(BF16)
    w_out_a16 = w_out_a[0].astype(BF16)
    w_kv16 = w_kv_b.astype(BF16)
    w_q16 = w_q_b[0].astype(BF16)
    w_out_b16 = w_out_b[0].astype(BF16)
    w_up16 = w_up.astype(BF16)
    w_down16 = w_down.astype(BF16)

    cos_p, sin_p = _rope_tables(jnp.arange(sp))
    cos_s, sin_s = _rope_tables(jnp.full((ns,), past_len, jnp.int32))

    cache_kt_a = jnp.transpose(cache_k_a.reshape(n_pool, page, N_HEADS, HEAD_DIM), (0, 2, 3, 1))
    cache_vt_a = jnp.transpose(cache_v_a.reshape(n_pool, page, N_HEADS, HEAD_DIM), (0, 2, 3, 1))
    cache_lft_a = jnp.transpose(cache_logf_a.reshape(n_pool, page, N_HEADS), (0, 2, 1))
    cache_kt_b = jnp.transpose(cache_k_b, (0, 2, 3, 1))
    cache_vt_b = jnp.transpose(cache_v_b, (0, 2, 3, 1))

    sh, sc, gt, shf, scf, gtf = mod_p[0]
    q_pm, k_pm, v_pm, ka_p, va_p, lfa_p, drow = _fox_proj_prompt(
        x_prompt, g_norm[0, 0], sc, sh, w_qkv_a, w_f_a, b_f_a[0])
    o_p = _fox_attn_prompt(q_pm, k_pm, v_pm, drow)
    h_p, cv_p0 = _post(x_prompt, o_p, w_out_a16, g_norm[0], gt, scf, shf, gtf,
                       w_up16[0], w_conv[0], b_conv[0], w_down16[0])

    sh_s, sc_s, gt_s, shf_s, scf_s, gtf_s = mod_s[0]
    xs = x_sample.reshape(ns, d)
    q_s, ka_s, va_s, lfa_s = _fox_proj_sample(
        xs, g_norm[0, 0], sc_s.reshape(ns, d), sh_s.reshape(ns, d), w_qkv_a, w_f_a, b_f_a[0])
    o_s = _fox_attn_sample(q_s, ka_s, va_s, lfa_s, cache_kt_a, cache_vt_a, cache_lft_a, page_table)
    prev0 = (state_conv[0][:, 0].reshape(1, ns, -1), state_conv[0][:, 1].reshape(1, ns, -1))
    h_s, a_s0 = _post(xs.reshape(1, ns, d), o_s.reshape(1, ns, d), w_out_a16, g_norm[0], gt_s, scf_s, shf_s, gtf_s,
                      w_up16[0], w_conv[0], b_conv[0], w_down16[0], prev=prev0)

    sh, sc, gt, shf, scf, gtf = mod_p[1]
    q_cls, k_cls, v_cls, kb_p, vb_p = _kvq(h_p, g_kv, g_norm[1, 0], sc, sh, w_kv16, w_q16, cos_p, sin_p,
                                           keep_rows=min(W_MAX, sp), dils=DILATIONS)
    sh_s, sc_s, gt_s, shf_s, scf_s, gtf_s = mod_s[1]
    (qb_s,), _, _, kb_s, vb_s = _kvq(h_s, g_kv, g_norm[1, 0], sc_s, sh_s, w_kv16, w_q16, cos_s, sin_s, keep_rows=ns)

    o1_p, lse_p = _dilated_prompt(q_cls, k_cls, v_cls)
    y_p, cv_p1 = _post(h_p, o1_p, w_out_b16, g_norm[1], gt, scf, shf, gtf,
                       w_up16[1], w_conv[1], b_conv[1], w_down16[1], lse=lse_p)
    o1_s = _dilated_sample(qb_s.reshape(ns, d), kb_s.reshape(ns, d), vb_s.reshape(ns, d), cache_kt_b, cache_vt_b)
    prev1 = (state_conv[1][:, 0].reshape(1, ns, -1), state_conv[1][:, 1].reshape(1, ns, -1))
    y_s, a_s1 = _post(h_s, o1_s.reshape(1, ns, d), w_out_b16, g_norm[1], gt_s, scf_s, shf_s, gtf_s,
                      w_up16[1], w_conv[1], b_conv[1], w_down16[1], prev=prev1)

    heads = (N_HEADS, HEAD_DIM)
    new_conv_sample = jnp.stack([
        jnp.stack([state_conv[0][:, 1], a_s0.reshape(ns, -1)], axis=1),
        jnp.stack([state_conv[1][:, 1], a_s1.reshape(ns, -1)], axis=1),
    ])
    return (
        y_p,
        y_s.reshape(ns, 1, d),
        ka_p.reshape(1, bp, sp, *heads),
        va_p.reshape(1, bp, sp, *heads),
        lfa_p.reshape(1, bp, sp, N_HEADS),
        ka_s.reshape(1, ns, 1, *heads),
        va_s.reshape(1, ns, 1, *heads),
        lfa_s.reshape(1, ns, 1, N_HEADS),
        kb_p.reshape(bp, -1, *heads),
        vb_p.reshape(bp, -1, *heads),
        kb_s.reshape(ns, 1, *heads),
        vb_s.reshape(ns, 1, *heads),
        jnp.stack([cv_p0, cv_p1]),
        new_conv_sample,
    )
```

```python
import functools

import jax
import jax.numpy as jnp
from jax import lax
from jax.experimental import pallas as pl
from jax.experimental.pallas import tpu as pltpu

F32 = jnp.float32
BF16 = jnp.bfloat16

N_HEADS = 16
HEAD_DIM = 64
DILATIONS = (1, 4, 16)
WINDOW_KEYS = 128
W_MAX = 2048
ROPE_THETA = 10000.0
NORM_EPS = 1e-6
NEG_INF = -1e30
SCALE = HEAD_DIM ** -0.5
LOG2_E = 1.4426950408889634

LANES = 128
SUBLANES = 8
PAIR = 2 * HEAD_DIM
VMEM_LIMIT_BYTES = 60 * 1024 * 1024
DOWN_PIECES = 2

_NT = (((1,), (1,)), ((), ()))
_TN = (((0,), (0,)), ((), ()))


def _params(n_grid_axes):
    return pltpu.CompilerParams(
        dimension_semantics=("arbitrary",) * n_grid_axes,
        vmem_limit_bytes=VMEM_LIMIT_BYTES,
    )


def _resident(shape):
    nd = len(shape)
    return pl.BlockSpec(shape, lambda *_: (0,) * nd, pipeline_mode=pl.Buffered(1))


def _rms_hat(x):
    return x * lax.rsqrt(jnp.mean(x * x, axis=-1, keepdims=True) + NORM_EPS)


def _log_sigmoid(x):
    return jnp.minimum(x, 0.0) - jnp.log1p(jnp.exp(-jnp.abs(x)))


def _gelu_tanh(x):
    return x * (0.5 * (1.0 + jnp.tanh(0.7978845608028654 * (x + 0.044715 * (x * x * x)))))


def _split3(x):
    hi = x.astype(BF16)
    r = x - hi.astype(F32)
    mid = r.astype(BF16)
    lo = (r - mid.astype(F32)).astype(BF16)
    return hi, mid, lo


def _dot(a, b):
    return jnp.dot(a, b, preferred_element_type=F32)


def _rope(x, cos, sin):
    d = x.shape[-1]
    lane = lax.broadcasted_iota(jnp.int32, x.shape, 1)
    first = (lane % HEAD_DIM) < (HEAD_DIM // 2)
    partner = jnp.where(first, pltpu.roll(x, d - HEAD_DIM // 2, 1), pltpu.roll(x, HEAD_DIM // 2, 1))
    reps = d // LANES
    return x * jnp.tile(cos, (1, reps)) + partner * jnp.tile(sin, (1, reps))


def _rope_tables(pos):
    half = HEAD_DIM // 2
    inv = jnp.power(ROPE_THETA, -jnp.arange(half, dtype=F32) * (2.0 / HEAD_DIM))
    ang = pos.astype(F32)[:, None] * inv[None, :]
    cos, sin = jnp.cos(ang), jnp.sin(ang)
    cos_h = jnp.concatenate([cos, cos], axis=-1)
    sin_h = jnp.concatenate([-sin, sin], axis=-1)
    return jnp.tile(cos_h, (1, LANES // HEAD_DIM)), jnp.tile(sin_h, (1, LANES // HEAD_DIM))


def _col_rep(row):
    d = row.shape[1]
    first_row = lax.broadcasted_iota(jnp.int32, (SUBLANES, LANES), 0) == 0
    e0 = jnp.where(first_row, 1.0, 0.0).astype(BF16)
    col = lax.dot_general(jnp.broadcast_to(row, (SUBLANES, d)), e0, _TN, preferred_element_type=F32)
    return col.reshape(d // HEAD_DIM, HEAD_DIM, LANES)


def _cols_to_rows(cols):
    lane = lax.broadcasted_iota(jnp.int32, (HEAD_DIM, LANES), 1)
    packed = jnp.zeros((HEAD_DIM, LANES), F32)
    for j, col in enumerate(cols):
        packed = jnp.where(lane == j, col, packed)
    eye = (lax.broadcasted_iota(jnp.int32, (HEAD_DIM, HEAD_DIM), 0)
           == lax.broadcasted_iota(jnp.int32, (HEAD_DIM, HEAD_DIM), 1))
    eye = jnp.where(eye, 1.0, 0.0).astype(BF16)
    rows = lax.dot_general(packed.astype(BF16), eye, _TN, preferred_element_type=F32)
    return rows[0:len(cols)]


def _mod_body(c_ref, w_ref, b_ref, o_ref):
    c = c_ref[...]
    s = (c * jax.nn.sigmoid(c)).astype(BF16)
    o_ref[0] = _dot(s, w_ref[0].astype(BF16)) + b_ref[0]


def _modulation(c_all, w_mod, b_mod):
    depth, d, n = w_mod.shape
    r = c_all.shape[0]
    tn = 1536
    return pl.pallas_call(
        _mod_body,
        grid=(depth, n // tn),
        in_specs=[
            pl.BlockSpec((r, d), lambda l, j: (0, 0)),
            pl.BlockSpec((1, d, tn), lambda l, j: (l, 0, j)),
            pl.BlockSpec((1, 1, tn), lambda l, j: (l, 0, j)),
        ],
        out_specs=pl.BlockSpec((1, r, tn), lambda l, j: (l, 0, j)),
        out_shape=jax.ShapeDtypeStruct((depth, r, n), F32),
        compiler_params=_params(2),
        name="modulation",
    )(c_all, w_mod, b_mod.reshape(depth, 1, n))


def _fox_proj_prompt_body(x_ref, g_ref, sc_ref, sh_ref, w_ref, wf_ref, wft_ref, bf_ref, bft_ref, tri_ref,
                          q_ref, k_ref, v_ref, kf_ref, vf_ref, lf_ref, d_ref, carry_ref):
    i = pl.program_id(1)
    x = x_ref[0]
    tm, dm = x.shape
    u = _rms_hat(x) * g_ref[...] * (1.0 + sc_ref[0]) + sh_ref[0]
    ub = u.astype(BF16)
    qkv = _dot(ub, w_ref[...])
    k = qkv[:, dm:2 * dm]
    v = qkv[:, 2 * dm:]
    kf_ref[0] = k
    vf_ref[0] = v
    qb = (qkv[:, :dm] * (SCALE * LOG2_E)).astype(BF16)
    kb = k.astype(BF16)
    vb = v.astype(BF16)
    for p in range(dm // PAIR):
        sl = slice(p * PAIR, (p + 1) * PAIR)
        q_ref[0, p] = qb[:, sl]
        k_ref[0, p] = kb[:, sl]
        v_ref[0, p] = vb[:, sl]
    lf_ref[0] = _log_sigmoid(_dot(ub, wf_ref[...]) + bf_ref[...])
    lft = _log_sigmoid(lax.dot_general(wft_ref[...], ub, _NT, preferred_element_type=F32) + bft_ref[...])
    tri = tri_ref[...]
    hi, mid, lo = _split3(lft)
    dloc = _dot(hi, tri) + _dot(mid, tri) + _dot(lo, tri)

    @pl.when(i == 0)
    def _():
        carry_ref[...] = jnp.zeros_like(carry_ref)

    dfull = dloc + carry_ref[...]
    carry_ref[...] = dfull[:, tm - 1:tm]
    for h in range(N_HEADS):
        d_ref[0, h] = dfull[h:h + 1, :]


def _fox_proj_prompt(x, g, sc, sh, w_qkv, w_f, b_f, tm=512):
    b, s, d = x.shape
    npair = d // PAIR
    nh = w_f.shape[1]
    tri = (jnp.arange(tm)[:, None] <= jnp.arange(tm)[None, :]).astype(BF16)
    pm_shape = jax.ShapeDtypeStruct((b, npair, s, PAIR), BF16)
    pm_spec = pl.BlockSpec((1, npair, tm, PAIR), lambda bi, i: (bi, 0, i, 0))
    row_spec = pl.BlockSpec((1, tm, d), lambda bi, i: (bi, i, 0))
    mod_spec = pl.BlockSpec((1, 1, d), lambda bi, i: (bi, 0, 0))
    return pl.pallas_call(
        _fox_proj_prompt_body,
        grid=(b, s // tm),
        in_specs=[
            row_spec, _resident((1, d)), mod_spec, mod_spec,
            _resident((d, 3 * d)), _resident((d, nh)), _resident((nh, d)),
            _resident((1, nh)), _resident((nh, 1)), _resident((tm, tm)),
        ],
        out_specs=[
            pm_spec, pm_spec, pm_spec, row_spec, row_spec,
            pl.BlockSpec((1, tm, nh), lambda bi, i: (bi, i, 0)),
            pl.BlockSpec((1, nh, 1, tm), lambda bi, i: (bi, 0, 0, i)),
        ],
        out_shape=[
            pm_shape, pm_shape, pm_shape,
            jax.ShapeDtypeStruct((b, s, d), F32), jax.ShapeDtypeStruct((b, s, d), F32),
            jax.ShapeDtypeStruct((b, s, nh), F32), jax.ShapeDtypeStruct((b, nh, 1, s), F32),
        ],
        scratch_shapes=[pltpu.VMEM((nh, 1), F32)],
        compiler_params=_params(2),
        name="fox_proj_prompt",
    )(x, g.reshape(1, d), sc, sh, w_qkv, w_f, w_f.T, b_f.reshape(1, nh), b_f.reshape(nh, 1), tri)


def _fox_proj_sample_body(x_ref, g_ref, sc_ref, sh_ref, w_ref, wf_ref, bf_ref, q_ref, k_ref, v_ref, lf_ref):
    x = x_ref[...]
    dm = x.shape[1]
    u = _rms_hat(x) * g_ref[...] * (1.0 + sc_ref[...]) + sh_ref[...]
    ub = u.astype(BF16)
    qkv = _dot(ub, w_ref[...])
    q_ref[...] = (qkv[:, :dm] * SCALE).astype(BF16)
    k_ref[...] = qkv[:, dm:2 * dm]
    v_ref[...] = qkv[:, 2 * dm:]
    lf_ref[...] = _log_sigmoid(_dot(ub, wf_ref[...]) + bf_ref[...])


def _fox_proj_sample(x, g, sc, sh, w_qkv, w_f, b_f):
    n, d = x.shape
    nh = w_f.shape[1]
    return pl.pallas_call(
        _fox_proj_sample_body,
        out_shape=[
            jax.ShapeDtypeStruct((n, d), BF16), jax.ShapeDtypeStruct((n, d), F32),
            jax.ShapeDtypeStruct((n, d), F32), jax.ShapeDtypeStruct((n, nh), F32),
        ],
        compiler_params=pltpu.CompilerParams(vmem_limit_bytes=VMEM_LIMIT_BYTES),
        name="fox_proj_sample",
    )(x, g.reshape(1, d), sc, sh, w_qkv, w_f, b_f.reshape(1, nh))


def _fox_attn_body(qi_ref, kj_ref, q_ref, k_ref, v_ref, dk_ref, dq_ref, o_ref, m_ref, acc_ref):
    t = pl.program_id(1)
    qi = qi_ref[t]
    kj = kj_ref[t]
    npair, tq = q_ref.shape[1], q_ref.shape[2]
    tk = k_ref.shape[2]
    lo_half = lax.broadcasted_iota(jnp.int32, (tq, PAIR), 1) < HEAD_DIM
    lo_half_k = lax.broadcasted_iota(jnp.int32, (tk, PAIR), 1) < HEAD_DIM

    @pl.when(kj == 0)
    def _():
        m_ref[...] = jnp.full(m_ref.shape, NEG_INF, F32)
        acc_ref[...] = jnp.zeros_like(acc_ref)

    def run(masked):
        if masked:
            keep = (lax.broadcasted_iota(jnp.int32, (tq, tk), 1)
                    <= lax.broadcasted_iota(jnp.int32, (tq, tk), 0))

        def pair_body(p, carry):
            qp = q_ref[0, p]
            kp = k_ref[0, p]
            vp = v_ref[0, p]
            for hh in range(2):
                h = 2 * p + hh
                sel = lo_half if hh == 0 else jnp.logical_not(lo_half)
                sel_k = lo_half_k if hh == 0 else jnp.logical_not(lo_half_k)
                qh = jnp.where(sel, qp, jnp.zeros_like(qp))
                s = lax.dot_general(qh, kp, _NT, preferred_element_type=F32)
                s = s + (dq_ref[0, h][:, 0:1] - dk_ref[0, h]) * LOG2_E
                if masked:
                    s = jnp.where(keep, s, NEG_INF)
                m_prev = m_ref[h]
                m_new = jnp.maximum(m_prev, jnp.max(s, axis=1, keepdims=True))
                alpha = jnp.exp2(m_prev - m_new)
                pr = jnp.exp2(s - jnp.tile(m_new, (1, tk // LANES)))
                m_ref[h] = m_new
                vh = jnp.where(sel_k, vp, jnp.ones_like(vp))
                acc_ref[h] = acc_ref[h] * alpha + _dot(pr.astype(BF16), vh)
            return carry

        lax.fori_loop(0, npair, pair_body, 0, unroll=4)

    @pl.when(kj < qi)
    def _():
        run(False)

    @pl.when(kj == qi)
    def _():
        run(True)
        for p in range(npair):
            a0 = acc_ref[2 * p]
            a1 = acc_ref[2 * p + 1]
            num = jnp.where(lo_half, a0, a1)
            den = pltpu.roll(jnp.where(lo_half, a1, a0), HEAD_DIM, 1)
            o_ref[0, :, p * PAIR:(p + 1) * PAIR] = (num / den).astype(BF16)


def _fox_attn_prompt(q_pm, k_pm, v_pm, drow, t=512):
    b, npair, s, _ = q_pm.shape
    nh = drow.shape[1]
    nt = s // t
    pairs = [(i, j) for i in range(nt) for j in range(i + 1)]
    qi = jnp.asarray([p[0] for p in pairs], jnp.int32)
    kj = jnp.asarray([p[1] for p in pairs], jnp.int32)
    grid_spec = pltpu.PrefetchScalarGridSpec(
        num_scalar_prefetch=2,
        grid=(b, len(pairs)),
        in_specs=[
            pl.BlockSpec((1, npair, t, PAIR), lambda bi, n, qi, kj: (bi, 0, qi[n], 0)),
            pl.BlockSpec((1, npair, t, PAIR), lambda bi, n, qi, kj: (bi, 0, kj[n], 0)),
            pl.BlockSpec((1, npair, t, PAIR), lambda bi, n, qi, kj: (bi, 0, kj[n], 0)),
            pl.BlockSpec((1, nh, 1, t), lambda bi, n, qi, kj: (bi, 0, 0, kj[n])),
            pl.BlockSpec((1, nh, 1, LANES), lambda bi, n, qi, kj: (bi, 0, 0, qi[n] * (t // LANES))),
        ],
        out_specs=pl.BlockSpec((1, t, npair * PAIR), lambda bi, n, qi, kj: (bi, qi[n], 0)),
        scratch_shapes=[pltpu.VMEM((nh, t, LANES), F32), pltpu.VMEM((nh, t, PAIR), F32)],
    )
    return pl.pallas_call(
        _fox_attn_body,
        grid_spec=grid_spec,
        out_shape=jax.ShapeDtypeStruct((b, s, npair * PAIR), BF16),
        compiler_params=_params(2),
        name="fox_attn_prompt",
    )(qi, kj, q_pm, k_pm, v_pm, drow, drow)


def _fox_decode_body(pt_ref, q_ref, kn_ref, vn_ref, lfn_ref, *refs, pages):
    k_refs = refs[:pages]
    v_refs = refs[pages:2 * pages]
    lf_refs = refs[2 * pages:3 * pages]
    o_ref, qb_ref, m_ref, l_ref, acc_ref, carry_ref, s_scr = refs[3 * pages:]
    g = pl.program_id(1)
    nh, page = lf_refs[0].shape[1], lf_refs[0].shape[2]
    lane = lax.broadcasted_iota(jnp.int32, (nh, page), 1)

    @pl.when(g == 0)
    def _():
        qb_ref[...] = _col_rep(q_ref[0])
        m_ref[...] = jnp.full(m_ref.shape, NEG_INF, F32)
        l_ref[...] = jnp.zeros_like(l_ref)
        acc_ref[...] = jnp.zeros_like(acc_ref)
        carry_ref[...] = jnp.zeros_like(carry_ref)

    for u in reversed(range(pages)):
        lft = lf_refs[u][0]
        suf = lft
        shift = 1
        while shift < page:
            suf = suf + jnp.where(lane + shift < page, pltpu.roll(suf, page - shift, 1), 0.0)
            shift *= 2
        carry = carry_ref[...]
        s_scr[:, u * page:(u + 1) * page] = (suf - lft) + carry
        carry_ref[...] = carry + suf[:, 0:1]
    for h in range(nh):
        qh = qb_ref[h]
        for u in range(pages):
            sl = slice(u * page, (u + 1) * page)
            s_scr[h:h + 1, sl] = s_scr[h:h + 1, sl] + jnp.sum(k_refs[u][0, h] * qh, axis=0, keepdims=True)
    s = s_scr[...]
    m_prev = m_ref[...]
    m_new = jnp.maximum(m_prev, jnp.max(s, axis=1, keepdims=True))
    alpha = jnp.exp(m_prev - m_new)
    pr = jnp.exp(s - m_new)
    l_ref[...] = alpha * l_ref[...] + jnp.sum(pr, axis=1, keepdims=True)
    m_ref[...] = m_new
    for h in range(nh):
        acc = acc_ref[h] * alpha[h:h + 1, :]
        for u in range(pages):
            acc = acc + pr[h:h + 1, u * page:(u + 1) * page] * v_refs[u][0, h]
        acc_ref[h] = acc

    @pl.when(g == pl.num_programs(1) - 1)
    def _():
        knb = _col_rep(kn_ref[0].astype(BF16))
        vnb = _col_rep(vn_ref[0].astype(BF16))
        for h in range(nh):
            s_scr[h:h + 1, 0:page] = jnp.sum(knb[h] * qb_ref[h], axis=0, keepdims=True)
        s_new = jnp.where(lane == 0, s_scr[:, 0:page] - lfn_ref[0], NEG_INF)
        m_old = m_ref[...]
        m_fin = jnp.maximum(m_old, jnp.max(s_new, axis=1, keepdims=True))
        a_fin = jnp.exp(m_old - m_fin)
        p_new = jnp.exp(s_new - m_fin)
        l_fin = a_fin * l_ref[...] + jnp.sum(p_new, axis=1, keepdims=True)
        cols = []
        for h in range(nh):
            acc = acc_ref[h] * a_fin[h:h + 1, :] + p_new[h:h + 1, :] * vnb[h]
            cols.append(jnp.sum(acc, axis=1, keepdims=True) / l_fin[h:h + 1, :])
        o_ref[0] = _cols_to_rows(cols).astype(BF16)


def _fox_attn_sample(q, k_new, v_new, lf_new, cache_kt, cache_vt, cache_lft, page_table, pages=16):
    n, d = q.shape
    _, nh, hd, page = cache_kt.shape
    n_groups = page_table.shape[1] // pages

    def page_map(nd, u):
        return lambda bi, g, pt: (pt[bi, (n_groups - 1 - g) * pages + u],) + (0,) * (nd - 1)

    seq3 = lambda bi, g, pt: (bi, 0, 0)
    kv_specs = [pl.BlockSpec((1, nh, hd, page), page_map(4, u)) for u in range(pages)]
    lf_specs = [pl.BlockSpec((1, nh, page), page_map(3, u)) for u in range(pages)]
    grid_spec = pltpu.PrefetchScalarGridSpec(
        num_scalar_prefetch=1,
        grid=(n, n_groups),
        in_specs=[
            pl.BlockSpec((1, 1, d), seq3), pl.BlockSpec((1, 1, d), seq3), pl.BlockSpec((1, 1, d), seq3),
            pl.BlockSpec((1, nh, 1), seq3),
        ] + kv_specs + kv_specs + lf_specs,
        out_specs=pl.BlockSpec((1, nh, hd), seq3),
        scratch_shapes=[
            pltpu.VMEM((nh, hd, LANES), F32), pltpu.VMEM((nh, 1), F32), pltpu.VMEM((nh, 1), F32),
            pltpu.VMEM((nh, hd, LANES), F32), pltpu.VMEM((nh, 1), F32), pltpu.VMEM((nh, pages * page), F32),
        ],
    )
    out = pl.pallas_call(
        functools.partial(_fox_decode_body, pages=pages),
        grid_spec=grid_spec,
        out_shape=jax.ShapeDtypeStruct((n, nh, hd), BF16),
        compiler_params=_params(2),
        name="fox_attn_sample",
    )(page_table, q.reshape(n, 1, d), k_new.reshape(n, 1, d), v_new.reshape(n, 1, d), lf_new.reshape(n, nh, 1),
      *([cache_kt] * pages), *([cache_vt] * pages), *([cache_lft] * pages))
    return out.reshape(n, d)


def _post_body(*refs, prompt, chunk, n_br):
    x_ref = refs[0]
    o_refs = refs[1:1 + n_br]
    refs = refs[1 + n_br:]
    if n_br > 1:
        lse_refs = refs[:n_br]
        e_ref = refs[n_br]
        refs = refs[n_br + 1:]
    wo_ref, g_ref, gt_ref, scf_ref, shf_ref, gtf_ref, wup_ref, wc_ref, bc_ref, wdn_ref = refs[:10]
    refs = refs[10:]
    if prompt:
        h_ref, cv_ref, a_scr, act_scr = refs[:4]
    else:
        p0_ref, p1_ref, h_ref, cv_ref, act_scr = refs[:5]
    x = x_ref[0]
    tm, dm = x.shape
    dff = wdn_ref.shape[0]
    g = g_ref[...]
    if n_br == 1:
        o = o_refs[0][0]
    else:
        nat_scr, lse_scr = refs[-2:]
        nslab = dm // LANES
        outs, lses = [], []
        ci = 0
        for dil, o_ref, lse_ref in zip(DILATIONS, o_refs, lse_refs):
            if dil == 1:
                outs.append(o_ref[0, 0].astype(F32))
                lses.append(lse_ref[0, 0])
                continue
            for r in range(dil):
                lse_scr[ci, pl.ds(r, tm // dil, stride=dil), :] = lse_ref[0, r]
                for j in range(nslab):
                    rows = o_ref[0, r, :, j * LANES:(j + 1) * LANES].astype(F32)
                    nat_scr[ci, j, pl.ds(r, tm // dil, stride=dil), :] = rows
            outs.append(jnp.concatenate([nat_scr[ci, j] for j in range(nslab)], axis=1))
            lses.append(lse_scr[ci])
            ci += 1
        top = functools.reduce(jnp.maximum, lses)
        es = [jnp.exp2(l - top) for l in lses]
        den = functools.reduce(lambda a, b: a + b, es)
        o = None
        for e, o_g in zip(es, outs):
            term = _dot((e / den).astype(BF16), e_ref[...]) * o_g
            o = term if o is None else o + term
        o = o.astype(BF16)
    att = _dot(o, wo_ref[...])
    h1 = x + gt_ref[0] * (_rms_hat(att) * g[1:2])
    u = (_rms_hat(h1) * g[2:3] * (1.0 + scf_ref[0]) + shf_ref[0]).astype(BF16)
    if prompt:
        i = pl.program_id(1)
        halo = a_scr.shape[0] - tm

        @pl.when(i == 0)
        def _():
            a_scr[0:halo, :] = jnp.zeros((halo, a_scr.shape[1]), F32)

    n_chunks = dff // chunk
    flush_at = [((k + 1) * n_chunks) // DOWN_PIECES - 1 for k in range(DOWN_PIECES)]
    ffn = None
    start = 0
    for c in range(n_chunks):
        ys = []
        for base in (0, dff):
            sl = slice(base + c * chunk, base + (c + 1) * chunk)
            a = _dot(u, wup_ref[:, sl])
            if prompt:
                a_scr[halo:halo + tm, sl] = a
                am2 = a_scr[halo - 2:halo - 2 + tm, sl]
                am1 = a_scr[halo - 1:halo - 1 + tm, sl]
            else:
                cv_ref[0, :, sl] = a
                am2 = p0_ref[0, :, sl]
                am1 = p1_ref[0, :, sl]
            ys.append(bc_ref[:, sl] + wc_ref[0:1, sl] * am2 + wc_ref[1:2, sl] * am1 + wc_ref[2:3, sl] * a)
        act_scr[:, c * chunk:(c + 1) * chunk] = (_gelu_tanh(ys[0]) * ys[1]).astype(BF16)
        if c in flush_at:
            stop = (c + 1) * chunk
            part = _dot(act_scr[:, start:stop], wdn_ref[start:stop, :])
            ffn = part if ffn is None else ffn + part
            start = stop
    if prompt:
        @pl.when(i == pl.num_programs(1) - 1)
        def _():
            cv_ref[0] = a_scr[halo + tm - 2:halo + tm, :]

        a_scr[0:halo, :] = a_scr[tm:tm + halo, :]
    h_ref[0] = h1 + gtf_ref[0] * (_rms_hat(ffn) * g[3:4])


def _post_rows(s, d, dff, n_br, prompt):
    resident = 2 * (d * d + d * 2 * dff + dff * d) + 4 * (4 * d + 4 * 2 * dff)
    n_cls = sum(1 for dil in DILATIONS if dil > 1) if n_br > 1 else 0
    for tm in (512, 256, 128, 64, 32, 16, 8):
        if s % tm:
            continue
        tiles = 2 * (tm * d * 4) * 2 + 2 * n_br * tm * d * 2
        scratch = tm * dff * 2 + ((tm + SUBLANES) * 2 * dff * 4 if prompt else 4 * tm * 2 * dff * 4)
        mix = (2 * n_br * tm * LANES * 4 + n_cls * (tm * d * 4 + tm * LANES * 4)) if n_br > 1 else 0
        live = 3 * tm * d * 4
        if resident + tiles + scratch + mix + live <= VMEM_LIMIT_BYTES:
            return tm
    raise ValueError("post kernel does not fit VMEM")


def _post(x, o, w_out, g4, gt, scf, shf, gtf, w_up, w_conv, b_conv, w_down, lse=None, prev=None, chunk=256):
    b, s, d = x.shape
    dff = w_down.shape[0]
    prompt = prev is None
    o_list = list(o) if isinstance(o, (list, tuple)) else [o]
    n_br = len(o_list)
    tm = _post_rows(s, d, dff, n_br, prompt)
    halo = SUBLANES
    row = lambda bi, i: (bi, i, 0)

    def mod_spec(a):
        per_row = a.shape[1] == s and s > 1
        return pl.BlockSpec((1, tm if per_row else 1, d), row if per_row else (lambda bi, i: (bi, 0, 0)))

    in_specs = [pl.BlockSpec((1, tm, d), row)]
    args = [x] + o_list
    scratch_mix = []
    if n_br == 1:
        in_specs += [pl.BlockSpec((1, tm, d), row)]
    else:
        head_of_lane = jnp.arange(d)[None, :] // HEAD_DIM
        expand = (jnp.arange(LANES)[:, None] == head_of_lane).astype(BF16)
        cls = lambda bi, i: (bi, 0, i, 0)
        in_specs += [pl.BlockSpec((1, dil, tm // dil, d), cls) for dil in DILATIONS]
        in_specs += [pl.BlockSpec((1, dil, tm // dil, LANES), cls) for dil in DILATIONS]
        in_specs += [_resident((LANES, d))]
        args += list(lse) + [expand]
        n_cls = sum(1 for dil in DILATIONS if dil > 1)
        scratch_mix = [pltpu.VMEM((n_cls, d // LANES, tm, LANES), F32), pltpu.VMEM((n_cls, tm, LANES), F32)]
    in_specs += [
        _resident((d, d)), _resident((4, d)), mod_spec(gt), mod_spec(scf), mod_spec(shf), mod_spec(gtf),
        _resident((d, 2 * dff)), _resident((3, 2 * dff)), _resident((1, 2 * dff)), _resident((dff, d)),
    ]
    args += [w_out, g4, gt, scf, shf, gtf, w_up, w_conv, b_conv.reshape(1, 2 * dff), w_down]
    scratch = [pltpu.VMEM((tm, dff), BF16)]
    if prompt:
        cv_shape = jax.ShapeDtypeStruct((b, 2, 2 * dff), F32)
        cv_spec = pl.BlockSpec((1, 2, 2 * dff), lambda bi, i: (bi, 0, 0))
        scratch = [pltpu.VMEM((tm + halo, 2 * dff), F32)] + scratch
    else:
        cv_shape = jax.ShapeDtypeStruct((b, s, 2 * dff), F32)
        cv_spec = pl.BlockSpec((1, tm, 2 * dff), row)
        in_specs += [pl.BlockSpec((1, tm, 2 * dff), row)] * 2
        args += [prev[0], prev[1]]
    return pl.pallas_call(
        functools.partial(_post_body, prompt=prompt, chunk=chunk, n_br=n_br),
        grid=(b, s // tm),
        in_specs=in_specs,
        out_specs=[pl.BlockSpec((1, tm, d), row), cv_spec],
        out_shape=[jax.ShapeDtypeStruct((b, s, d), F32), cv_shape],
        scratch_shapes=scratch + scratch_mix,
        compiler_params=_params(2),
        name="post_prompt" if prompt else "post_sample",
    )(*args)


def _kvq_body(x_ref, gkv_ref, gq_ref, sc_ref, sh_ref, wkv_ref, wq_ref, cos_ref, sin_ref, *refs, n_skip, dils):
    nd = len(dils)
    q_refs, k_refs, v_refs = refs[:nd], refs[nd:2 * nd], refs[2 * nd:3 * nd]
    kf_ref, vf_ref = refs[3 * nd:3 * nd + 2]
    i = pl.program_id(1)
    x = x_ref[0]
    tm, dm = x.shape
    xh = _rms_hat(x)
    cos = cos_ref[...]
    sin = sin_ref[...]
    kv = _dot((xh * gkv_ref[...]).astype(BF16), wkv_ref[...])
    k = _rope(kv[:, :dm], cos, sin)
    v = kv[:, dm:]
    uq = (xh * gq_ref[...] * (1.0 + sc_ref[0]) + sh_ref[0]).astype(BF16)
    q = _rope(_dot(uq, wq_ref[...]), cos, sin) * (SCALE * LOG2_E)

    @pl.when(i >= n_skip)
    def _():
        kf_ref[0] = k
        vf_ref[0] = v

    nslab = dm // LANES
    for val, outs in ((q, q_refs), (k, k_refs), (v, v_refs)):
        outs[0][0, 0] = val.astype(BF16)
        if nd > 1:
            stages = refs[3 * nd + 2:]
            for j in range(nslab):
                stages[0][0, j] = val[:, j * LANES:(j + 1) * LANES]
            for lvl in range(1, nd):
                step = dils[lvl] // dils[lvl - 1]
                n_rows = tm // dils[lvl]
                for r_prev in range(dils[lvl - 1]):
                    for r_sub in range(step):
                        r = r_prev + dils[lvl - 1] * r_sub
                        for j in range(nslab):
                            rows = stages[lvl - 1][r_prev, j, pl.ds(r_sub, n_rows, stride=step), :]
                            outs[lvl][0, r, :, j * LANES:(j + 1) * LANES] = rows.astype(BF16)
                            if lvl + 1 < nd:
                                stages[lvl][r, j] = rows


def _kvq(x, g_kv, g_q, sc, sh, w_kv, w_q, cos, sin, keep_rows, dils=(1,), tm=256):
    b, s, d = x.shape
    tm = min(tm, s)
    assert dils[0] == 1 and all(nxt % prv == 0 for prv, nxt in zip(dils, dils[1:]))
    n_skip = (s - keep_rows) // tm
    row = lambda bi, i: (bi, i, 0)
    per_row = sc.shape[1] == s and s > 1
    mod_spec = pl.BlockSpec((1, tm if per_row else 1, d), row if per_row else (lambda bi, i: (bi, 0, 0)))
    tab_spec = pl.BlockSpec((tm, LANES), lambda bi, i: (i, 0))
    kept = pl.BlockSpec((1, tm, d), lambda bi, i: (bi, jnp.maximum(i - n_skip, 0), 0))
    f32_kept = jax.ShapeDtypeStruct((b, keep_rows, d), F32)
    cls_specs = [pl.BlockSpec((1, dil, tm // dil, d), lambda bi, i: (bi, 0, i, 0)) for dil in dils]
    cls_shapes = [jax.ShapeDtypeStruct((b, dil, s // dil, d), BF16) for dil in dils]
    res = pl.pallas_call(
        functools.partial(_kvq_body, n_skip=n_skip, dils=tuple(dils)),
        grid=(b, s // tm),
        in_specs=[
            pl.BlockSpec((1, tm, d), row), _resident((1, d)), _resident((1, d)), mod_spec, mod_spec,
            _resident((d, 2 * d)), _resident((d, d)), tab_spec, tab_spec,
        ],
        out_specs=cls_specs * 3 + [kept, kept],
        out_shape=cls_shapes * 3 + [f32_kept, f32_kept],
        scratch_shapes=[pltpu.VMEM((dil, d // LANES, tm // dil, LANES), F32) for dil in dils[:-1]],
        compiler_params=_params(2),
        name="kvq_proj",
    )(x, g_kv.reshape(1, d), g_q.reshape(1, d), sc, sh, w_kv, w_q, cos, sin)
    nd = len(dils)
    return res[:nd], res[nd:2 * nd], res[2 * nd:3 * nd], res[3 * nd], res[3 * nd + 1]


def _dilated_body(q_ref, kp_ref, kc_ref, vp_ref, vc_ref, o_ref, lse_ref):
    i = pl.program_id(2)
    tq, dm = q_ref.shape
    a = lax.broadcasted_iota(jnp.int32, (tq, 2 * tq), 0)
    c = lax.broadcasted_iota(jnp.int32, (tq, 2 * tq), 1)
    keep = (c >= a) & (c <= a + tq) & ((c >= tq) | (i > 0))
    keep2 = jnp.concatenate([keep, keep], axis=0)
    lo_half = lax.broadcasted_iota(jnp.int32, (tq, PAIR), 1) < HEAD_DIM
    q = q_ref[...]
    kcat = jnp.concatenate([kp_ref[...], kc_ref[...]], axis=0)
    vcat = jnp.concatenate([vp_ref[...], vc_ref[...]], axis=0)
    lse_ref[...] = jnp.zeros((tq, LANES), F32)
    for p in range(dm // PAIR):
        sl = slice(p * PAIR, (p + 1) * PAIR)
        qp, kp, vp = q[:, sl], kcat[:, sl], vcat[:, sl]
        zero = jnp.zeros_like(qp)
        q2 = jnp.concatenate([jnp.where(lo_half, qp, zero), jnp.where(lo_half, zero, qp)], axis=0)
        s = lax.dot_general(q2, kp, _NT, preferred_element_type=F32)
        s = jnp.where(keep2, s, NEG_INF)
        m = jnp.max(s, axis=1, keepdims=True)
        pr = jnp.exp2(s - m)
        l = jnp.sum(pr, axis=1, keepdims=True)
        pv = _dot(pr.astype(BF16), vp) / l
        lse = m + jnp.log2(l)
        o_ref[:, sl] = jnp.where(lo_half, pv[:tq], pv[tq:]).astype(BF16)
        lse_ref[:, 2 * p:2 * p + 1] = lse[:tq]
        lse_ref[:, 2 * p + 1:2 * p + 2] = lse[tq:]


def _dilated_prompt(qs, ks, vs):
    tq = WINDOW_KEYS
    outs, lses = [], []
    for dil, q, k, v in zip(DILATIONS, qs, ks, vs):
        b, _, rows, d = q.shape
        cur = lambda bi, r, i: (bi, r, i, 0)
        prv = lambda bi, r, i: (bi, r, jnp.maximum(i - 1, 0), 0)
        wide = pl.BlockSpec((None, None, tq, d), cur)
        wide_prev = pl.BlockSpec((None, None, tq, d), prv)
        o_g, lse_g = pl.pallas_call(
            _dilated_body,
            grid=(b, dil, rows // tq),
            in_specs=[wide, wide_prev, wide, wide_prev, wide],
            out_specs=[wide, pl.BlockSpec((None, None, tq, LANES), cur)],
            out_shape=[jax.ShapeDtypeStruct((b, dil, rows, d), BF16),
                       jax.ShapeDtypeStruct((b, dil, rows, LANES), F32)],
            compiler_params=_params(3),
            name=f"dilated_prompt_d{dil}",
        )(q, k, k, v, v)
        outs.append(o_g)
        lses.append(lse_g)
    return outs, lses


def _dilated_sample_body(q_ref, kn_ref, vn_ref, k_ref, v_ref, o_ref, qb_ref, knb_ref, vnb_ref):
    j = pl.program_id(1)
    hg, rows = k_ref.shape[1], k_ref.shape[3]

    @pl.when(j == 0)
    def _():
        qb_ref[...] = _col_rep(q_ref[0])
        knb_ref[...] = _col_rep(kn_ref[0].astype(BF16))
        vnb_ref[...] = _col_rep(vn_ref[0].astype(BF16))

    idx = lax.broadcasted_iota(jnp.int32, (1, rows), 1)
    mult = jnp.zeros((1, rows), F32)
    for dil in DILATIONS:
        in_branch = (idx >= rows - WINDOW_KEYS * dil) & (lax.rem(rows - idx, dil) == 0)
        mult = mult + jnp.where(in_branch, 1.0, 0.0)
    valid = mult > 0.0
    n_self = float(len(DILATIONS))
    cols = []
    for h in range(hg):
        qh = qb_ref[j * hg + h]
        s = jnp.sum(k_ref[0, h] * jnp.tile(qh, (1, rows // LANES)), axis=0, keepdims=True)
        s_self = jnp.sum(knb_ref[j * hg + h] * qh, axis=0, keepdims=True)[:, 0:1]
        m = jnp.maximum(jnp.max(jnp.where(valid, s, NEG_INF), axis=1, keepdims=True), s_self)
        pr = jnp.where(valid, mult * jnp.exp2(s - m), 0.0)
        p_self = n_self * jnp.exp2(s_self - m)
        l = jnp.sum(pr, axis=1, keepdims=True) + p_self
        col = jnp.sum(v_ref[0, h] * pr, axis=1, keepdims=True) + p_self * vnb_ref[j * hg + h][:, 0:1]
        cols.append(col / l)
    o_ref[0] = _cols_to_rows(cols).astype(BF16)


def _dilated_sample(q, k_new, v_new, cache_kt, cache_vt, hg=16):
    n, d = q.shape
    _, nh, hd, rows = cache_kt.shape
    seq3 = lambda bi, j: (bi, 0, 0)
    slab = pl.BlockSpec((1, hg, hd, rows), lambda bi, j: (bi, j, 0, 0))
    row_spec = pl.BlockSpec((1, 1, d), seq3)
    rep = pltpu.VMEM((nh, hd, LANES), F32)
    out = pl.pallas_call(
        _dilated_sample_body,
        grid=(n, nh // hg),
        in_specs=[row_spec, row_spec, row_spec, slab, slab],
        out_specs=pl.BlockSpec((1, hg, hd), lambda bi, j: (bi, j, 0)),
        out_shape=jax.ShapeDtypeStruct((n, nh, hd), BF16),
        scratch_shapes=[rep, rep, rep],
        compiler_params=_params(2),
        name="dilated_sample",
    )(q.reshape(n, 1, d), k_new.reshape(n, 1, d), v_new.reshape(n, 1, d), cache_kt, cache_vt)
    return out.reshape(n, d)


def kernel(x_prompt, x_sample, cache_k_a, cache_v_a, cache_logf_a, page_table, cache_k_b, cache_v_b, state_conv, c_prompt, c_sample, w_mod, b_mod, g_norm, w_in_a, b_f_a, w_out_a, g_kv, w_kv_b, w_q_b, w_out_b, w_up, w_conv, b_conv, w_down):
    bp, sp, d = x_prompt.shape
    ns = x_sample.shape[0]
    depth = w_mod.shape[0]
    assert depth == 2 and x_sample.shape[1] == 1 and d == N_HEADS * HEAD_DIM and cache_k_a.shape[0] == 1
    n_pool, page = cache_k_a.shape[1], cache_k_a.shape[2]
    past_len = page_table.shape[1] * page
    wb = cache_k_b.shape[1]
    assert wb == W_MAX and sp >= W_MAX

    pad = (-(bp + ns)) % SUBLANES
    c_all = jnp.concatenate([c_prompt, c_sample, jnp.zeros((pad, d), F32)], axis=0)
    mod = _modulation(c_all, w_mod, b_mod).reshape(depth, bp + ns + pad, 6, d)
    mod_p = [[mod[l, :bp, j].reshape(bp, 1, d) for j in range(6)] for l in range(depth)]
    mod_s = [[mod[l, bp:bp + ns, j].reshape(1, ns, d) for j in range(6)] for l in range(depth)]

    w_qkv_a = w_in_a[0, :, :3 * d].astype(BF16)
    w_f_a = w_in_a[0, :, 3 * d:].astype(BF16)
    w_out_a16 = w_out_a[0].astype(BF16)
    w_kv16 = w_kv_b.astype(BF16)
    w_q16 = w_q_b[0].astype(BF16)
    w_out_b16 = w_out_b[0].astype(BF16)
    w_up16 = w_up.astype(BF16)
    w_down16 = w_down.astype(BF16)

    cos_p, sin_p = _rope_tables(jnp.arange(sp))
    cos_s, sin_s = _rope_tables(jnp.full((ns,), past_len, jnp.int32))

    cache_kt_a = jnp.transpose(cache_k_a.reshape(n_pool, page, N_HEADS, HEAD_DIM), (0, 2, 3, 1))
    cache_vt_a = jnp.transpose(cache_v_a.reshape(n_pool, page, N_HEADS, HEAD_DIM), (0, 2, 3, 1))
    cache_lft_a = jnp.transpose(cache_logf_a.reshape(n_pool, page, N_HEADS), (0, 2, 1))
    cache_kt_b = jnp.transpose(cache_k_b, (0, 2, 3, 1))
    cache_vt_b = jnp.transpose(cache_v_b, (0, 2, 3, 1))

    sh, sc, gt, shf, scf, gtf = mod_p[0]
    q_pm, k_pm, v_pm, ka_p, va_p, lfa_p, drow = _fox_proj_prompt(
        x_prompt, g_norm[0, 0], sc, sh, w_qkv_a, w_f_a, b_f_a[0])
    o_p = _fox_attn_prompt(q_pm, k_pm, v_pm, drow)
    h_p, cv_p0 = _post(x_prompt, o_p, w_out_a16, g_norm[0], gt, scf, shf, gtf,
                       w_up16[0], w_conv[0], b_conv[0], w_down16[0])

    sh_s, sc_s, gt_s, shf_s, scf_s, gtf_s = mod_s[0]
    xs = x_sample.reshape(ns, d)
    q_s, ka_s, va_s, lfa_s = _fox_proj_sample(
        xs, g_norm[0, 0], sc_s.reshape(ns, d), sh_s.reshape(ns, d), w_qkv_a, w_f_a, b_f_a[0])
    o_s = _fox_attn_sample(q_s, ka_s, va_s, lfa_s, cache_kt_a, cache_vt_a, cache_lft_a, page_table)
    prev0 = (state_conv[0][:, 0].reshape(1, ns, -1), state_conv[0][:, 1].reshape(1, ns, -1))
    h_s, a_s0 = _post(xs.reshape(1, ns, d), o_s.reshape(1, ns, d), w_out_a16, g_norm[0], gt_s, scf_s, shf_s, gtf_s,
                      w_up16[0], w_conv[0], b_conv[0], w_down16[0], prev=prev0)

    sh, sc, gt, shf, scf, gtf = mod_p[1]
    q_cls, k_cls, v_cls, kb_p, vb_p = _kvq(h_p, g_kv, g_norm[1, 0], sc, sh, w_kv16, w_q16, cos_p, sin_p,
                                           keep_rows=min(W_MAX, sp), dils=DILATIONS)
    sh_s, sc_s, gt_s, shf_s, scf_s, gtf_s = mod_s[1]
    (qb_s,), _, _, kb_s, vb_s = _kvq(h_s, g_kv, g_norm[1, 0], sc_s, sh_s, w_kv16, w_q16, cos_s, sin_s, keep_rows=ns)

    o1_p, lse_p = _dilated_prompt(q_cls, k_cls, v_cls)
    y_p, cv_p1 = _post(h_p, o1_p, w_out_b16, g_norm[1], gt, scf, shf, gtf,
                       w_up16[1], w_conv[1], b_conv[1], w_down16[1], lse=lse_p)
    o1_s = _dilated_sample(qb_s.reshape(ns, d), kb_s.reshape(ns, d), vb_s.reshape(ns, d), cache_kt_b, cache_vt_b)
    prev1 = (state_conv[1][:, 0].reshape(1, ns, -1), state_conv[1][:, 1].reshape(1, ns, -1))
    y_s, a_s1 = _post(h_s, o1_s.reshape(1, ns, d), w_out_b16, g_norm[1], gt_s, scf_s, shf_s, gtf_s,
                      w_up16[1], w_conv[1], b_conv[1], w_down16[1], prev=prev1)

    heads = (N_HEADS, HEAD_DIM)
    new_conv_sample = jnp.stack([
        jnp.stack([state_conv[0][:, 1], a_s0.reshape(ns, -1)], axis=1),
        jnp.stack([state_conv[1][:, 1], a_s1.reshape(ns, -1)], axis=1),
    ])
    return (
        y_p,
        y_s.reshape(ns, 1, d),
        ka_p.reshape(1, bp, sp, *heads),
        va_p.reshape(1, bp, sp, *heads),
        lfa_p.reshape(1, bp, sp, N_HEADS),
        ka_s.reshape(1, ns, 1, *heads),
        va_s.reshape(1, ns, 1, *heads),
        lfa_s.reshape(1, ns, 1, N_HEADS),
        kb_p.reshape(bp, -1, *heads),
        vb_p.reshape(bp, -1, *heads),
        kb_s.reshape(ns, 1, *heads),
        vb_s.reshape(ns, 1, *heads),
        jnp.stack([cv_p0, cv_p1]),
        new_conv_sample,
    )
```
